```python
import math
import jax
import jax.numpy as jnp
from jax import lax
import numpy as np

D_MODEL = 2048
BATCH = 4
SEQ = 8192
DEPTH = 2
DEC_BATCH = 16
DEC_SEQ = 16
PAST_LEN = 2048

CHUNK = 64
A_HEADS = 16
A_HEAD_DIM = 64
A_WIDTH = A_HEADS * A_HEAD_DIM
A_BAND = 8
REL_MAX = 128
B_GROUPS = 8
B_GROUP_DIM = 128
B_WIDTH = B_GROUPS * B_GROUP_DIM
B_CHUNK = 128
C_QK_HEADS = 16
C_V_HEADS = 32
C_HEAD_DIM = 128
C_QK_WIDTH = C_QK_HEADS * C_HEAD_DIM
C_V_WIDTH = C_V_HEADS * C_HEAD_DIM
C_CONV = 4
C_CONV_CH = 2 * C_QK_WIDTH + C_V_WIDTH
AB_IN = 4 * A_WIDTH + 3 * B_WIDTH
C_IN = C_CONV_CH + C_V_WIDTH + 2 * C_V_HEADS
AB_SPLITS = [A_WIDTH, 2 * A_WIDTH, 3 * A_WIDTH, 4 * A_WIDTH, 4 * A_WIDTH + B_WIDTH, 4 * A_WIDTH + 2 * B_WIDTH]
C_SPLITS = [C_CONV_CH, C_CONV_CH + C_V_WIDTH, C_CONV_CH + C_V_WIDTH + C_V_HEADS]
DN_ALPHA = (2 * DEPTH) ** 0.25
DN_BETA = (8 * DEPTH) ** -0.25
LN_EPS = 1e-5
NORM_EPS = 1e-6

kernel_name = 'streaming_band_gmlp_deltanet_hybrid'


def layer_norm(x, g, b):
    xf = x.astype(jnp.float32)
    mu = jnp.mean(xf, axis=-1, keepdims=True)
    var = jnp.mean(jnp.square(xf - mu), axis=-1, keepdims=True)
    return ((xf - mu) * lax.rsqrt(var + LN_EPS) * g + b).astype(x.dtype)


def rms_norm(x, g):
    xf = x.astype(jnp.float32)
    return xf * lax.rsqrt(jnp.mean(xf * xf, axis=-1, keepdims=True) + NORM_EPS) * g


def l2_normalize(t):
    t = t.astype(jnp.float32)
    return t * lax.rsqrt(jnp.sum(t * t, axis=-1, keepdims=True) + NORM_EPS)


def rel_bias(table, n_q, n_k, offset):
    dist = jnp.arange(n_q)[:, None] + offset - jnp.arange(n_k)[None, :]
    return table[:, jnp.clip(dist, -REL_MAX, REL_MAX) + REL_MAX]


def band_attention(q, k, v, bias, valid):
    s = jnp.einsum('bnqhd,bnkhd->bnhqk', q, k, preferred_element_type=jnp.float32) * (A_HEAD_DIM ** -0.5)
    s = s + bias.astype(jnp.float32)
    if valid is not None:
        s = jnp.where(valid[None, :, None, None, :], s, -jnp.inf)
    p = jax.nn.softmax(s, axis=-1)
    return jnp.einsum('bnhqk,bnkhd->bnqhd', p.astype(v.dtype), v)


def band_attention_prompt(q, k, v, rel_table):
    B, T, H, D = q.shape
    n_chunks = T // CHUNK
    pad = ((0, 0), (A_BAND * CHUNK, 0), (0, 0), (0, 0))
    kp = jnp.pad(k, pad).reshape(B, n_chunks + A_BAND, CHUNK, H, D)
    vp = jnp.pad(v, pad).reshape(B, n_chunks + A_BAND, CHUNK, H, D)
    band_idx = jnp.arange(n_chunks)[:, None] + jnp.arange(A_BAND + 1)[None, :]
    band_len = (A_BAND + 1) * CHUNK
    kb = kp[:, band_idx].reshape(B, n_chunks, band_len, H, D)
    vb = vp[:, band_idx].reshape(B, n_chunks, band_len, H, D)
    valid = jnp.repeat(band_idx - A_BAND >= 0, CHUNK, axis=1)
    bias = rel_bias(rel_table, CHUNK, band_len, A_BAND * CHUNK)
    o = band_attention(q.reshape(B, n_chunks, CHUNK, H, D), kb, vb, bias, valid)
    return o.reshape(B, T, H, D)


def band_attention_sample(q, k, v, cache_k, cache_v, rel_table):
    n_cache = cache_k.shape[1]
    L = q.shape[1]
    kk = jnp.concatenate([cache_k.astype(k.dtype), k], axis=1)[:, None]
    vv = jnp.concatenate([cache_v.astype(v.dtype), v], axis=1)[:, None]
    bias = rel_bias(rel_table, L, n_cache + L, n_cache)
    return band_attention(q[:, None], kk, vv, bias, None)[:, 0]


def chunk_token_mlp(u, v, w_s, b_s, n):
    B, T, W = v.shape
    vc = v.reshape(B, T // n, n, B_GROUPS, B_GROUP_DIM)
    w = jnp.tril(w_s[:, :n, :n])
    mix = jnp.einsum('gij,bnjgc->bnigc', w, vc) + b_s[:, :n].T[:, :, None]
    return u * mix.reshape(B, T, W)


def ab_mixer(h, cache_k, cache_v, w_in, rel_table, ln_v_g, ln_v_b, w_s, b_s, w_out):
    B, T, _ = h.shape
    q, k, v, z_a, u_b, v_b, z_b = jnp.split(h @ w_in, AB_SPLITS, axis=-1)
    q = q.reshape(B, T, A_HEADS, A_HEAD_DIM)
    k = k.reshape(B, T, A_HEADS, A_HEAD_DIM)
    v = v.reshape(B, T, A_HEADS, A_HEAD_DIM)
    if cache_k is None:
        o_a = band_attention_prompt(q, k, v, rel_table)
        keep = min(A_BAND * CHUNK, T)
        new_k, new_v = k[:, T - keep:], v[:, T - keep:]
    else:
        o_a = band_attention_sample(q, k, v, cache_k, cache_v, rel_table)
        new_k, new_v = k, v
    o_a = o_a.reshape(B, T, A_WIDTH) * jax.nn.silu(z_a)
    v_b = layer_norm(jax.nn.gelu(v_b, approximate=False), ln_v_g, ln_v_b)
    o_b = chunk_token_mlp(jax.nn.gelu(u_b, approximate=False), v_b, w_s, b_s, min(T, B_CHUNK)) * jax.nn.silu(z_b)
    y = jnp.concatenate([o_a, o_b], axis=-1) @ w_out
    return y, new_k, new_v, v_b.reshape(B, T, B_GROUPS, B_GROUP_DIM)


def causal_conv(x, left, w):
    xp = jnp.concatenate([left.astype(x.dtype), x], axis=1)
    y = lax.conv_general_dilated(xp, w[:, None, :].astype(x.dtype), window_strides=(1,), padding='VALID',
                                 dimension_numbers=('NWC', 'WIO', 'NWC'), feature_group_count=x.shape[-1])
    return y, xp[:, xp.shape[1] - (C_CONV - 1):]


def gated_delta_rule(q, k, v, log_a, beta, s0, block):
    B, T, H, K = q.shape
    n_blk = T // block

    def blk(a):
        return jnp.moveaxis(a.reshape(B, n_blk, block, H, *a.shape[3:]), 3, 2)

    q, k, v, log_a, beta = blk(q), blk(k), blk(v), blk(log_a), blk(beta)
    g = jnp.cumsum(log_a, axis=-1)
    incl = jnp.tril(jnp.ones((block, block), bool))
    strict = jnp.tril(jnp.ones((block, block), bool), -1)
    decay = jnp.exp(jnp.where(incl, g[..., :, None] - g[..., None, :], -jnp.inf))
    kb = k * beta[..., None]
    a_mat = jnp.where(strict, jnp.einsum('bnhik,bnhjk->bnhij', kb, k) * decay, 0.0)
    eye = jnp.eye(block, dtype=jnp.float32)
    t_inv = lax.linalg.triangular_solve(a_mat + eye, jnp.broadcast_to(eye, a_mat.shape),
                                        left_side=True, lower=True, unit_diagonal=True)
    w = jnp.einsum('bnhij,bnhjk->bnhik', t_inv, kb * jnp.exp(g)[..., None])
    u = jnp.einsum('bnhij,bnhjv->bnhiv', t_inv, v * beta[..., None])
    qk = jnp.einsum('bnhik,bnhjk->bnhij', q, k) * decay
    q_dec = q * jnp.exp(g)[..., None]
    g_last = g[..., -1]
    k_dec = k * jnp.exp(g_last[..., None] - g)[..., None]

    def step(s, xs):
        w_n, u_n, qk_n, qd_n, kd_n, gl_n = xs
        v_new = u_n - jnp.einsum('bhik,bhkv->bhiv', w_n, s)
        o = jnp.einsum('bhik,bhkv->bhiv', qd_n, s) + jnp.einsum('bhij,bhjv->bhiv', qk_n, v_new)
        s = s * jnp.exp(gl_n)[..., None, None] + jnp.einsum('bhik,bhiv->bhkv', kd_n, v_new)
        return s, o

    xs = (jnp.moveaxis(w, 1, 0), jnp.moveaxis(u, 1, 0), jnp.moveaxis(qk, 1, 0),
          jnp.moveaxis(q_dec, 1, 0), jnp.moveaxis(k_dec, 1, 0), jnp.moveaxis(g_last, 1, 0))
    s_final, o = lax.scan(step, s0, xs)
    o = jnp.moveaxis(jnp.moveaxis(o, 0, 1), 2, 3).reshape(B, T, H, v.shape[-1])
    return o, s_final


def dn_mixer(h, conv_left, s0, w_in, w_conv, a_log, dt_bias, o_norm_g, w_out):
    B, T, _ = h.shape
    qkv, z, b_raw, a_raw = jnp.split(h @ w_in, C_SPLITS, axis=-1)
    if conv_left is None:
        conv_left = jnp.zeros((B, C_CONV - 1, C_CONV_CH), h.dtype)
    if s0 is None:
        s0 = jnp.zeros((B, C_V_HEADS, C_HEAD_DIM, C_HEAD_DIM), jnp.float32)
    conv_y, new_conv = causal_conv(qkv, conv_left, w_conv)
    q, k, v = jnp.split(jax.nn.silu(conv_y), [C_QK_WIDTH, 2 * C_QK_WIDTH], axis=-1)
    rep = C_V_HEADS // C_QK_HEADS
    q = jnp.repeat(l2_normalize(q.reshape(B, T, C_QK_HEADS, C_HEAD_DIM)) * (C_HEAD_DIM ** -0.5), rep, axis=2)
    k = jnp.repeat(l2_normalize(k.reshape(B, T, C_QK_HEADS, C_HEAD_DIM)), rep, axis=2)
    v = v.reshape(B, T, C_V_HEADS, C_HEAD_DIM).astype(jnp.float32)
    beta = jax.nn.sigmoid(b_raw.astype(jnp.float32))
    log_a = -jnp.exp(a_log.astype(jnp.float32)) * jax.nn.softplus(a_raw.astype(jnp.float32) + dt_bias.astype(jnp.float32))
    o, s_new = gated_delta_rule(q, k, v, log_a, beta, s0.astype(jnp.float32), min(CHUNK, T))
    o = rms_norm(o, o_norm_g) * jax.nn.silu(z.reshape(B, T, C_V_HEADS, C_HEAD_DIM).astype(jnp.float32))
    y = o.reshape(B, T, C_V_WIDTH).astype(h.dtype) @ w_out
    return y, new_conv, s_new.astype(h.dtype)


def trunk(x, c, cache_a_k, cache_a_v, state_c_conv, state_c_s,
          w_ada, b_ada, ln_g, ln_b, w_in_ab, rel_table, ln_v_g, ln_v_b, w_s, b_s, w_out_ab,
          w_in_dn, w_conv, a_log, dt_bias, o_norm_g, w_out_dn):
    for layer in range(DEPTH):
        shift, scale, gate = jnp.split((c @ w_ada[layer] + b_ada[layer])[:, None, :], 3, axis=-1)
        h = x * (1 + scale) + shift
        if layer % 2 == 0:
            y, new_a_k, new_a_v, new_b_v = ab_mixer(h, cache_a_k, cache_a_v, w_in_ab, rel_table,
                                                    ln_v_g, ln_v_b, w_s, b_s, w_out_ab)
        else:
            y, new_c_conv, new_c_s = dn_mixer(h, state_c_conv, state_c_s, w_in_dn, w_conv, a_log,
                                              dt_bias, o_norm_g, w_out_dn)
        x = layer_norm(DN_ALPHA * x + (1 + gate) * y, ln_g[layer], ln_b[layer])
    return x, new_a_k, new_a_v, new_b_v, new_c_conv, new_c_s


def setup_inputs(seed: int = 0) -> dict:
    key = jax.random.key(seed)
    ks = jax.random.split(key, 26)

    def nrm(k, shape, s):
        return jax.random.normal(k, shape, jnp.float32) * s

    a_cache = min(A_BAND * CHUNK, PAST_LEN)
    dt = jnp.exp(jax.random.uniform(ks[22], (C_V_HEADS,), jnp.float32, math.log(1e-3), math.log(1e-1)))
    return {
        'x_prompt': nrm(ks[0], (BATCH, SEQ, D_MODEL), 1.0),
        'x_sample': nrm(ks[1], (DEC_BATCH, DEC_SEQ, D_MODEL), 1.0),
        'cache_a_k': nrm(ks[2], (DEC_BATCH, a_cache, A_HEADS, A_HEAD_DIM), 1.0),
        'cache_a_v': nrm(ks[3], (DEC_BATCH, a_cache, A_HEADS, A_HEAD_DIM), 1.0),
        'state_c_conv': nrm(ks[4], (DEC_BATCH, C_CONV - 1, C_CONV_CH), 1.0),
        'state_c_s': nrm(ks[5], (DEC_BATCH, C_V_HEADS, C_HEAD_DIM, C_HEAD_DIM), 0.1),
        'c_prompt': nrm(ks[6], (BATCH, D_MODEL), 1.0),
        'c_sample': nrm(ks[7], (DEC_BATCH, D_MODEL), 1.0),
        'w_ada': nrm(ks[8], (DEPTH, D_MODEL, 3 * D_MODEL), 0.1 * D_MODEL ** -0.5),
        'b_ada': nrm(ks[9], (DEPTH, 3 * D_MODEL), 0.01),
        'ln_g': 1.0 + nrm(ks[10], (DEPTH, D_MODEL), 0.05),
        'ln_b': nrm(ks[11], (DEPTH, D_MODEL), 0.02),
        'w_in_ab': nrm(ks[12], (D_MODEL, AB_IN), D_MODEL ** -0.5),
        'rel_table': nrm(ks[13], (A_HEADS, 2 * REL_MAX + 1), 0.5),
        'ln_v_g': 1.0 + nrm(ks[14], (B_WIDTH,), 0.05),
        'ln_v_b': nrm(ks[15], (B_WIDTH,), 0.02),
        'w_s': nrm(ks[16], (B_GROUPS, B_CHUNK, B_CHUNK), B_CHUNK ** -0.5),
        'b_s': 1.0 + nrm(ks[17], (B_GROUPS, B_CHUNK), 0.1),
        'w_out_ab': nrm(ks[18], (A_WIDTH + B_WIDTH, D_MODEL), DN_BETA * (A_WIDTH + B_WIDTH) ** -0.5),
        'w_in_dn': nrm(ks[19], (D_MODEL, C_IN), D_MODEL ** -0.5),
        'w_conv': nrm(ks[20], (C_CONV, C_CONV_CH), C_CONV ** -0.5),
        'a_log': jnp.log(jax.random.uniform(ks[21], (C_V_HEADS,), jnp.float32, 1.0, 16.0)),
        'dt_bias': dt + jnp.log(-jnp.expm1(-dt)),
        'o_norm_g': 1.0 + nrm(ks[23], (C_HEAD_DIM,), 0.05),
        'w_out_dn': nrm(ks[24], (C_V_WIDTH, D_MODEL), DN_BETA * C_V_WIDTH ** -0.5),
    }


def reference(x_prompt, x_sample, cache_a_k, cache_a_v, state_c_conv, state_c_s, c_prompt, c_sample,
              w_ada, b_ada, ln_g, ln_b, w_in_ab, rel_table, ln_v_g, ln_v_b, w_s, b_s, w_out_ab,
              w_in_dn, w_conv, a_log, dt_bias, o_norm_g, w_out_dn):
    weights = (w_ada, b_ada, ln_g, ln_b, w_in_ab, rel_table, ln_v_g, ln_v_b, w_s, b_s, w_out_ab,
               w_in_dn, w_conv, a_log, dt_bias, o_norm_g, w_out_dn)
    y_prompt, p_a_k, p_a_v, _, p_c_conv, p_c_s = trunk(x_prompt, c_prompt, None, None, None, None, *weights)
    y_sample, s_a_k, s_a_v, s_b_v, s_c_conv, s_c_s = trunk(x_sample, c_sample, cache_a_k, cache_a_v,
                                                           state_c_conv, state_c_s, *weights)
    return (y_prompt, y_sample, p_a_k, p_a_v, p_c_conv, p_c_s, s_a_k, s_a_v, s_b_v, s_c_conv, s_c_s)
```

```python
import functools

import jax
import jax.numpy as jnp
from jax import lax
from jax.experimental import pallas as pl
from jax.experimental.pallas import tpu as pltpu

F32 = jnp.float32
BF16 = jnp.bfloat16

D_MODEL = 2048
DEPTH = 2
CHUNK = 64
A_HEADS = 16
A_HEAD_DIM = 64
A_WIDTH = A_HEADS * A_HEAD_DIM
A_BAND = 8
REL_MAX = 128
B_GROUPS = 8
B_GROUP_DIM = 128
B_WIDTH = B_GROUPS * B_GROUP_DIM
B_CHUNK = 128
C_QK_HEADS = 16
C_V_HEADS = 32
C_HEAD_DIM = 128
C_QK_WIDTH = C_QK_HEADS * C_HEAD_DIM
C_V_WIDTH = C_V_HEADS * C_HEAD_DIM
C_CONV = 4
C_CONV_CH = 2 * C_QK_WIDTH + C_V_WIDTH
C_MAIN = C_CONV_CH + C_V_WIDTH
DN_ALPHA = (2 * DEPTH) ** 0.25
LN_EPS = 1e-5
NORM_EPS = 1e-6
NEG_BIG = -1e30

LANES = 128
HALO_ROWS = 8
VMEM_LIMIT = 56 * 1024 * 1024


def _cparams(sem):
    return pltpu.CompilerParams(dimension_semantics=sem, vmem_limit_bytes=VMEM_LIMIT)


def _silu(x):
    return x * jax.nn.sigmoid(x)


def _gelu(x):
    return 0.5 * x * (1.0 + lax.erf(x * (2.0 ** -0.5)))


def _layer_norm(x, g, b):
    mu = jnp.mean(x, axis=-1, keepdims=True)
    xc = x - mu
    var = jnp.mean(xc * xc, axis=-1, keepdims=True)
    return xc * lax.rsqrt(var + LN_EPS) * g + b


def _ada_kernel(c_ref, w_ref, b_ref, o_ref):
    c = c_ref[...].astype(BF16)
    w = w_ref[0].astype(BF16)
    o_ref[0] = jnp.dot(c, w, preferred_element_type=F32) + b_ref[0]


def _ada(c_all, w_ada, b_ada):
    rows, d = c_all.shape
    depth, _, n = w_ada.shape
    tn = 768
    return pl.pallas_call(
        _ada_kernel,
        grid=(depth, n // tn),
        in_specs=[
            pl.BlockSpec((rows, d), lambda l, j: (0, 0)),
            pl.BlockSpec((1, d, tn), lambda l, j: (l, 0, j)),
            pl.BlockSpec((1, 1, tn), lambda l, j: (l, 0, j)),
        ],
        out_specs=pl.BlockSpec((1, rows, tn), lambda l, j: (l, 0, j)),
        out_shape=jax.ShapeDtypeStruct((depth, rows, n), F32),
        compiler_params=_cparams(("parallel", "parallel")),
        name="ada",
    )(c_all, w_ada, b_ada.reshape(depth, 1, n))


def _inproj_kernel(x_ref, mod_ref, w_ref, *rest, has_extra):
    if has_extra:
        we_ref, o_ref, oe_ref, h_scr = rest
    else:
        o_ref, h_scr = rest
    j = pl.program_id(2)

    @pl.when(j == 0)
    def _():
        h = x_ref[0] * (1.0 + mod_ref[0, 1:2, :]) + mod_ref[0, 0:1, :]
        hb = h.astype(BF16)
        h_scr[...] = hb
        if has_extra:
            oe_ref[0] = jnp.dot(hb, we_ref[...], preferred_element_type=F32)

    o_ref[0] = jnp.dot(h_scr[...], w_ref[...], preferred_element_type=F32).astype(o_ref.dtype)


def _inproj(x, mod, w, w_extra=None, *, tm, tn=512):
    b, t, d = x.shape
    n = w.shape[1]
    tm = min(tm, t)
    has_extra = w_extra is not None
    in_specs = [
        pl.BlockSpec((1, tm, d), lambda bb, i, j: (bb, i, 0)),
        pl.BlockSpec((1, 3, d), lambda bb, i, j: (bb, 0, 0)),
        pl.BlockSpec((d, tn), lambda bb, i, j: (0, j)),
    ]
    out_specs = [pl.BlockSpec((1, tm, tn), lambda bb, i, j: (bb, i, j))]
    out_shape = [jax.ShapeDtypeStruct((b, t, n), BF16)]
    args = [x, mod, w]
    if has_extra:
        ne = w_extra.shape[1]
        in_specs.append(pl.BlockSpec((d, ne), lambda bb, i, j: (0, 0)))
        out_specs.append(pl.BlockSpec((1, tm, ne), lambda bb, i, j: (bb, i, 0)))
        out_shape.append(jax.ShapeDtypeStruct((b, t, ne), F32))
        args.append(w_extra)
    res = pl.pallas_call(
        functools.partial(_inproj_kernel, has_extra=has_extra),
        grid=(b, t // tm, n // tn),
        in_specs=in_specs,
        out_specs=out_specs,
        out_shape=out_shape,
        scratch_shapes=[pltpu.VMEM((tm, d), BF16)],
        compiler_params=_cparams(("parallel", "parallel", "arbitrary")),
        name="inproj",
    )(*args)
    return res if has_extra else res[0]


def _attn_kernel(q_ref, k0_ref, k1_ref, k2_ref, v0_ref, v1_ref, v2_ref, z_ref, bias_ref, o_ref, *, tq):
    i = pl.program_id(2)
    q = q_ref[0]
    k = jnp.concatenate([k0_ref[0], k1_ref[0], k2_ref[0]], axis=0)
    v = jnp.concatenate([v0_ref[0], v1_ref[0], v2_ref[0]], axis=0)
    col = lax.broadcasted_iota(jnp.int32, (1, 3 * tq), 1)
    missing = col < (2 - i) * tq
    outs = []
    for hh in range(2):
        sl = slice(hh * A_HEAD_DIM, (hh + 1) * A_HEAD_DIM)
        s = lax.dot_general(q[:, sl], k[:, sl], (((1,), (1,)), ((), ())), preferred_element_type=F32)
        s = s * (A_HEAD_DIM ** -0.5) + bias_ref[hh]
        s = jnp.where(missing, NEG_BIG, s)
        m = jnp.max(s, axis=-1, keepdims=True)
        p = jnp.exp(s - m)
        l = jnp.sum(p, axis=-1, keepdims=True)
        o = jnp.dot(p.astype(BF16), v[:, sl], preferred_element_type=F32)
        outs.append(o / l)
    o = jnp.concatenate(outs, axis=1)
    o_ref[0] = (o * _silu(z_ref[0].astype(F32))).astype(o_ref.dtype)


def _band_bias_blocks(rel_table, tq):
    band_len = (A_BAND + 1) * CHUNK
    dist = jnp.arange(CHUNK)[:, None] + A_BAND * CHUNK - jnp.arange(band_len)[None, :]
    cb = rel_table[:, jnp.clip(dist, -REL_MAX, REL_MAX) + REL_MAX].astype(F32)
    nq = tq // CHUNK
    lead = 2 * tq - A_BAND * CHUNK
    rows = []
    for r in range(nq):
        left = lead + r * CHUNK
        right = 3 * tq - left - band_len
        rows.append(jnp.pad(cb, ((0, 0), (0, 0), (left, right)), constant_values=NEG_BIG))
    return jnp.concatenate(rows, axis=1)


def _attn_prompt(p0, bias, *, tq=256):
    b, t, _ = p0.shape
    npair = A_HEADS // 2
    qo, ko, vo, zo = 0, A_WIDTH // LANES, 2 * A_WIDTH // LANES, 3 * A_WIDTH // LANES

    def blk(off, back):
        return pl.BlockSpec((1, tq, LANES), lambda hp, bb, i: (bb, jnp.maximum(i - back, 0), off + hp))

    return pl.pallas_call(
        functools.partial(_attn_kernel, tq=tq),
        grid=(npair, b, t // tq),
        in_specs=[blk(qo, 0), blk(ko, 2), blk(ko, 1), blk(ko, 0), blk(vo, 2), blk(vo, 1), blk(vo, 0), blk(zo, 0),
                  pl.BlockSpec((2, tq, 3 * tq), lambda hp, bb, i: (hp, 0, 0))],
        out_specs=pl.BlockSpec((1, tq, LANES), lambda hp, bb, i: (bb, i, hp)),
        out_shape=jax.ShapeDtypeStruct((b, t, A_WIDTH), BF16),
        compiler_params=_cparams(("parallel", "parallel", "arbitrary")),
        name="attn_prompt",
    )(p0, p0, p0, p0, p0, p0, p0, p0, bias)


def _attn_sample_kernel(q_ref, k_ref, v_ref, z_ref, ck_ref, cv_ref, bias_ref, o_ref, *, pad_rows):
    q = q_ref[0]
    zpad = jnp.zeros((pad_rows, A_WIDTH), BF16)
    k = jnp.concatenate([ck_ref[0], k_ref[0], zpad], axis=0)
    v = jnp.concatenate([cv_ref[0], v_ref[0], zpad], axis=0)
    outs = []
    for h in range(A_HEADS):
        sl = slice(h * A_HEAD_DIM, (h + 1) * A_HEAD_DIM)
        s = lax.dot_general(q[:, sl], k[:, sl], (((1,), (1,)), ((), ())), preferred_element_type=F32)
        s = s * (A_HEAD_DIM ** -0.5) + bias_ref[h]
        m = jnp.max(s, axis=-1, keepdims=True)
        p = jnp.exp(s - m)
        l = jnp.sum(p, axis=-1, keepdims=True)
        o = jnp.dot(p.astype(BF16), v[:, sl], preferred_element_type=F32)
        outs.append(o / l)
    o = jnp.concatenate(outs, axis=1)
    o_ref[0] = (o * _silu(z_ref[0].astype(F32))).astype(o_ref.dtype)


def _attn_sample(p0, cache_k, cache_v, rel_table):
    b, t, _ = p0.shape
    n_cache = cache_k.shape[1]
    n_keys = n_cache + t
    n_pad = -n_keys % LANES
    dist = jnp.arange(t)[:, None] + n_cache - jnp.arange(n_keys)[None, :]
    bias = rel_table[:, jnp.clip(dist, -REL_MAX, REL_MAX) + REL_MAX].astype(F32)
    bias = jnp.pad(bias, ((0, 0), (0, 0), (0, n_pad)), constant_values=NEG_BIG)
    ck = cache_k.reshape(b, n_cache, A_WIDTH).astype(BF16)
    cv = cache_v.reshape(b, n_cache, A_WIDTH).astype(BF16)

    def col(j):
        return pl.BlockSpec((1, t, A_WIDTH), lambda bb: (bb, 0, j))

    cache_spec = pl.BlockSpec((1, n_cache, A_WIDTH), lambda bb: (bb, 0, 0))
    return pl.pallas_call(
        functools.partial(_attn_sample_kernel, pad_rows=n_pad),
        grid=(b,),
        in_specs=[col(0), col(1), col(2), col(3), cache_spec, cache_spec,
                  pl.BlockSpec((A_HEADS, t, n_keys + n_pad), lambda bb: (0, 0, 0))],
        out_specs=pl.BlockSpec((1, t, A_WIDTH), lambda bb: (bb, 0, 0)),
        out_shape=jax.ShapeDtypeStruct((b, t, A_WIDTH), BF16),
        compiler_params=_cparams(("parallel",)),
        name="attn_sample",
    )(p0, p0, p0, p0, ck, cv, bias)


def _gmlp_kernel(u_ref, v_ref, z_ref, ws_ref, bst_ref, g_ref, b_ref, o_ref, *vn_out, n_mix, tb):
    vn = _layer_norm(_gelu(v_ref[0].astype(F32)), g_ref[...], b_ref[...])
    if vn_out:
        vn_out[0][0] = vn
    z = z_ref[0].astype(F32)
    gate = _gelu(u_ref[0].astype(F32)) * _silu(z)
    row = lax.broadcasted_iota(jnp.int32, (B_CHUNK, B_CHUNK), 0)
    colm = lax.broadcasted_iota(jnp.int32, (B_CHUNK, B_CHUNK), 1)
    keep = (colm <= row) & (row < n_mix)
    rows = min(tb, B_CHUNK)
    for g in range(B_GROUPS):
        gs = slice(g * B_GROUP_DIM, (g + 1) * B_GROUP_DIM)
        w = jnp.where(keep, ws_ref[g], 0.0).astype(BF16)
        bias = bst_ref[:, g:g + 1]
        for c in range(max(tb // B_CHUNK, 1)):
            rs = slice(c * B_CHUNK, c * B_CHUNK + rows)
            vg = vn[rs, gs].astype(BF16)
            if rows < B_CHUNK:
                vg = jnp.concatenate([vg, jnp.zeros((B_CHUNK - rows, B_GROUP_DIM), BF16)], axis=0)
            mix = jnp.dot(w, vg, preferred_element_type=F32) + bias
            o_ref[0, rs, gs] = (gate[rs, gs] * mix[:rows]).astype(o_ref.dtype)


def _gmlp(p0, w_s, b_s, ln_v_g, ln_v_b, *, tb, want_vn):
    b, t, _ = p0.shape
    tb = min(tb, t)
    n_mix = min(t, B_CHUNK)
    base = 4 * A_WIDTH // B_WIDTH

    def col(j):
        return pl.BlockSpec((1, tb, B_WIDTH), lambda bb, i: (bb, i, base + j))

    out_specs = [pl.BlockSpec((1, tb, B_WIDTH), lambda bb, i: (bb, i, 0))]
    out_shape = [jax.ShapeDtypeStruct((b, t, B_WIDTH), BF16)]
    if want_vn:
        out_specs.append(pl.BlockSpec((1, tb, B_WIDTH), lambda bb, i: (bb, i, 0)))
        out_shape.append(jax.ShapeDtypeStruct((b, t, B_WIDTH), F32))
    res = pl.pallas_call(
        functools.partial(_gmlp_kernel, n_mix=n_mix, tb=tb),
        grid=(b, t // tb),
        in_specs=[col(0), col(1), col(2),
                  pl.BlockSpec((B_GROUPS, B_CHUNK, B_CHUNK), lambda bb, i: (0, 0, 0)),
                  pl.BlockSpec((B_CHUNK, B_GROUPS), lambda bb, i: (0, 0)),
                  pl.BlockSpec((1, B_WIDTH), lambda bb, i: (0, 0)),
                  pl.BlockSpec((1, B_WIDTH), lambda bb, i: (0, 0))],
        out_specs=out_specs,
        out_shape=out_shape,
        compiler_params=_cparams(("parallel", "parallel")),
        name="gmlp",
    )(p0, p0, p0, w_s, b_s.T, ln_v_g.reshape(1, -1), ln_v_b.reshape(1, -1))
    return res if want_vn else res[0]


def _outproj_kernel(*refs, widths):
    o_refs = refs[:len(widths)]
    w_ref, x_ref, mod_ref, g_ref, b_ref, out_ref = refs[len(widths):]
    y = None
    off = 0
    for o_ref, kw in zip(o_refs, widths):
        part = jnp.dot(o_ref[0], w_ref[off:off + kw, :], preferred_element_type=F32)
        y = part if y is None else y + part
        off += kw
    r = DN_ALPHA * x_ref[0] + (1.0 + mod_ref[0, 2:3, :]) * y
    out_ref[0] = _layer_norm(r, g_ref[...], b_ref[...])


def _outproj(os_, w, x, mod, ln_g, ln_b, *, tm):
    b, t, d = x.shape
    tm = min(tm, t)
    widths = tuple(o.shape[-1] for o in os_)
    ktot = sum(widths)
    in_specs = [pl.BlockSpec((1, tm, kw), lambda bb, i: (bb, i, 0)) for kw in widths]
    in_specs += [
        pl.BlockSpec((ktot, d), lambda bb, i: (0, 0), pipeline_mode=pl.Buffered(1)),
        pl.BlockSpec((1, tm, d), lambda bb, i: (bb, i, 0)),
        pl.BlockSpec((1, 3, d), lambda bb, i: (bb, 0, 0)),
        pl.BlockSpec((1, d), lambda bb, i: (0, 0)),
        pl.BlockSpec((1, d), lambda bb, i: (0, 0)),
    ]
    return pl.pallas_call(
        functools.partial(_outproj_kernel, widths=widths),
        grid=(b, t // tm),
        in_specs=in_specs,
        out_specs=pl.BlockSpec((1, tm, d), lambda bb, i: (bb, i, 0)),
        out_shape=jax.ShapeDtypeStruct((b, t, d), F32),
        compiler_params=_cparams(("parallel", "parallel")),
        name="outproj",
    )(*os_, w, x, mod, ln_g.reshape(1, d), ln_b.reshape(1, d))


def _delta_kernel(q_ref, k_ref, v_ref, z_ref, ba_ref, hq0_ref, hk0_ref, hv0_ref, wq_ref, wk_ref, wv_ref,
                  ad_ref, gn_ref, s0_ref, o_ref, s_ref, hq_scr, hk_scr, hv_scr, gt_scr,
                  *, heads, nblk, blk, t_valid):
    hg = pl.program_id(1)
    n = pl.program_id(2)
    tb = nblk * blk

    @pl.when(n == 0)
    def _():
        hq_scr[...] = hq0_ref[0]
        hk_scr[...] = hk0_ref[0]
        hv_scr[...] = hv0_ref[0]
        s_ref[...] = s0_ref[...]

    def conv_silu(raw_ref, halo_scr, w_ref):
        x = raw_ref[0].astype(F32)
        ext = jnp.concatenate([halo_scr[...], x], axis=0)
        y = w_ref[C_CONV - 1:C_CONV, :] * x
        for tap in range(C_CONV - 1):
            lo = HALO_ROWS - (C_CONV - 1) + tap
            y = y + w_ref[tap:tap + 1, :] * ext[lo:lo + tb]
        halo_scr[...] = x[tb - HALO_ROWS:tb]
        return _silu(y)

    qc = conv_silu(q_ref, hq_scr, wq_ref)
    kc = conv_silu(k_ref, hk_scr, wk_ref)
    vc = conv_silu(v_ref, hv_scr, wv_ref)

    def l2n(x):
        return x * lax.rsqrt(jnp.sum(x * x, axis=-1, keepdims=True) + NORM_EPS)

    qn, kn = [], []
    for j in range(heads // 2):
        hs = slice(j * C_HEAD_DIM, (j + 1) * C_HEAD_DIM)
        qn.append(l2n(qc[:, hs]) * (C_HEAD_DIM ** -0.5))
        kn.append(l2n(kc[:, hs]))

    ba = ba_ref[0]
    beta_all = jax.nn.sigmoid(ba)
    xa = ba + ad_ref[1:2, :]
    softplus = jnp.maximum(xa, 0.0) + jnp.log1p(jnp.exp(-jnp.abs(xa)))
    la_all = -jnp.exp(ad_ref[0:1, :]) * softplus
    if t_valid < tb:
        rowv = lax.broadcasted_iota(jnp.int32, (tb, LANES), 0) < t_valid
        beta_all = jnp.where(rowv, beta_all, 0.0)
        la_all = jnp.where(rowv, la_all, 0.0)

    lane = lax.broadcasted_iota(jnp.int32, (blk, LANES), 1)
    ri = lax.broadcasted_iota(jnp.int32, (blk, blk), 0)
    ci = lax.broadcasted_iota(jnp.int32, (blk, blk), 1)
    incl = ci <= ri
    strict = ci < ri
    ltri = incl.astype(F32)
    eye = (ci == ri).astype(F32)
    gn = gn_ref[...]

    for nb in range(nblk):
        rs = slice(nb * blk, (nb + 1) * blk)
        g_all = jnp.dot(ltri, la_all[rs], preferred_element_type=F32, precision=lax.Precision.HIGHEST)
        gt_scr[...] = g_all.T
        beta_blk = beta_all[rs]
        for hh in range(heads):
            hidx = hg * heads + hh
            beta_c = jnp.sum(jnp.where(lane == hidx, beta_blk, 0.0), axis=1, keepdims=True)
            g_c = jnp.sum(jnp.where(lane == C_V_HEADS + hidx, g_all, 0.0), axis=1, keepdims=True)
            g_r = gt_scr[pl.ds(C_V_HEADS + hidx, 1), :]
            g_last = g_c[blk - 1:blk, :]
            dec = jnp.where(incl, jnp.exp(g_c - g_r), 0.0)
            eg = jnp.exp(g_c)
            q = qn[hh // 2][rs]
            k = kn[hh // 2][rs]
            v = vc[rs, hh * C_HEAD_DIM:(hh + 1) * C_HEAD_DIM]
            kb = k * beta_c
            kbf = k.astype(BF16)
            qa = lax.dot_general(jnp.concatenate([q, kb], axis=0).astype(BF16), kbf,
                                 (((1,), (1,)), ((), ())), preferred_element_type=F32)
            qk = qa[:blk] * dec
            a = jnp.where(strict, qa[blk:] * dec, 0.0)
            tinv = eye - a
            pw = a
            span = 2
            while span < blk:
                pwb = pw.astype(BF16)
                pw = jnp.dot(pwb, pwb, preferred_element_type=F32)
                tinv = tinv + jnp.dot(tinv.astype(BF16), pw.astype(BF16), preferred_element_type=F32)
                span *= 2
            rhs = jnp.concatenate([kb * eg, v * beta_c], axis=1).astype(BF16)
            wu = jnp.dot(tinv.astype(BF16), rhs, preferred_element_type=F32)
            w = wu[:, :C_HEAD_DIM]
            u = wu[:, C_HEAD_DIM:]
            s = s_ref[0, hh]
            ws_qs = jnp.dot(jnp.concatenate([w, q * eg], axis=0).astype(BF16), s.astype(BF16),
                            preferred_element_type=F32)
            v_new = u - ws_qs[:blk]
            vnb = v_new.astype(BF16)
            o = ws_qs[blk:] + jnp.dot(qk.astype(BF16), vnb, preferred_element_type=F32)
            k_dec = (k * jnp.exp(g_last - g_c)).astype(BF16)
            s_ref[0, hh] = s * jnp.exp(g_last) + lax.dot_general(
                k_dec, vnb, (((0,), (0,)), ((), ())), preferred_element_type=F32)
            on = o * lax.rsqrt(jnp.mean(o * o, axis=-1, keepdims=True) + NORM_EPS) * gn
            hs = slice(hh * C_HEAD_DIM, (hh + 1) * C_HEAD_DIM)
            o_ref[0, rs, hs] = (on * _silu(z_ref[0, rs, hs].astype(F32))).astype(o_ref.dtype)


def _delta(p1, ba, conv_left8, s0, w_conv, ad_rows, o_norm_g, *, heads, nblk, t_valid):
    b, t, _ = p1.shape
    blk = CHUNK
    tb = nblk * blk
    cq = heads // 2 * C_HEAD_DIM
    cv = heads * C_HEAD_DIM
    ko, vo, zo = C_QK_WIDTH // cq, 2 * C_QK_WIDTH // cv, C_CONV_CH // cv

    def tok(width, off):
        return pl.BlockSpec((1, tb, width), lambda bb, hg, n: (bb, n, off + hg))

    def halo(width, off):
        return pl.BlockSpec((1, HALO_ROWS, width), lambda bb, hg, n: (bb, 0, off + hg))

    def wc(width, off):
        return pl.BlockSpec((C_CONV, width), lambda bb, hg, n: (0, off + hg))

    state_spec = pl.BlockSpec((1, heads, C_HEAD_DIM, C_HEAD_DIM), lambda bb, hg, n: (bb, hg, 0, 0))
    return pl.pallas_call(
        functools.partial(_delta_kernel, heads=heads, nblk=nblk, blk=blk, t_valid=t_valid),
        grid=(b, C_V_HEADS // heads, t // tb),
        in_specs=[tok(cq, 0), tok(cq, ko), tok(cv, vo), tok(cv, zo),
                  pl.BlockSpec((1, tb, LANES), lambda bb, hg, n: (bb, n, 0)),
                  halo(cq, 0), halo(cq, ko), halo(cv, vo),
                  wc(cq, 0), wc(cq, ko), wc(cv, vo),
                  pl.BlockSpec((2, LANES), lambda bb, hg, n: (0, 0)),
                  pl.BlockSpec((1, C_HEAD_DIM), lambda bb, hg, n: (0, 0)),
                  state_spec],
        out_specs=[pl.BlockSpec((1, tb, cv), lambda bb, hg, n: (bb, n, hg)), state_spec],
        out_shape=[jax.ShapeDtypeStruct((b, t, C_V_WIDTH), BF16),
                   jax.ShapeDtypeStruct((b, C_V_HEADS, C_HEAD_DIM, C_HEAD_DIM), F32)],
        scratch_shapes=[pltpu.VMEM((HALO_ROWS, cq), F32), pltpu.VMEM((HALO_ROWS, cq), F32),
                        pltpu.VMEM((HALO_ROWS, cv), F32), pltpu.VMEM((LANES, blk), F32)],
        compiler_params=_cparams(("parallel", "parallel", "arbitrary")),
        name="delta",
    )(p1, p1, p1, p1, ba, conv_left8, conv_left8, conv_left8, w_conv, w_conv, w_conv,
      ad_rows, o_norm_g.reshape(1, C_HEAD_DIM), s0)


def _trunk(x, mods, wts, cache_k, cache_v, conv_left, s0, *, prompt):
    (w_in_ab, bias_blocks, rel_table, ln_v_g, ln_v_b, w_s, b_s, w_out_ab, w_in_dn, w_extra, w_conv, ad_rows,
     o_norm_g, w_out_dn, ln_g, ln_b) = wts
    b, t, _ = x.shape
    p0 = _inproj(x, mods[0], w_in_ab, tm=1024)
    if prompt:
        o_a = _attn_prompt(p0, bias_blocks)
        o_b = _gmlp(p0, w_s, b_s, ln_v_g, ln_v_b, tb=256, want_vn=False)
        v_n = None
        keep = min(A_BAND * CHUNK, t)
    else:
        o_a = _attn_sample(p0, cache_k, cache_v, rel_table)
        o_b, v_n = _gmlp(p0, w_s, b_s, ln_v_g, ln_v_b, tb=256, want_vn=True)
        keep = t
    new_k = p0[:, t - keep:, A_WIDTH:2 * A_WIDTH].astype(F32).reshape(b, keep, A_HEADS, A_HEAD_DIM)
    new_v = p0[:, t - keep:, 2 * A_WIDTH:3 * A_WIDTH].astype(F32).reshape(b, keep, A_HEADS, A_HEAD_DIM)
    x1 = _outproj([o_a, o_b], w_out_ab, x, mods[0], ln_g[0], ln_b[0], tm=512)
    p1, ba = _inproj(x1, mods[1], w_in_dn, w_extra, tm=1024)
    new_conv = p1[:, t - (C_CONV - 1):, :C_CONV_CH].astype(F32)
    if conv_left is None:
        left8 = jnp.zeros((b, HALO_ROWS, C_CONV_CH), F32)
    else:
        left8 = jnp.pad(conv_left, ((0, 0), (HALO_ROWS - (C_CONV - 1), 0), (0, 0)))
    if s0 is None:
        s0 = jnp.zeros((b, C_V_HEADS, C_HEAD_DIM, C_HEAD_DIM), F32)
    t_pad = -t % CHUNK
    if t_pad:
        p1 = jnp.pad(p1, ((0, 0), (0, t_pad), (0, 0)))
        ba = jnp.pad(ba, ((0, 0), (0, t_pad), (0, 0)))
    o_c, s_new = _delta(p1, ba, left8, s0, w_conv, ad_rows, o_norm_g,
                        heads=4, nblk=2 if (t + t_pad) % (2 * CHUNK) == 0 else 1, t_valid=t if t_pad else t + t_pad)
    if t_pad:
        o_c = o_c[:, :t]
    x2 = _outproj([o_c], w_out_dn, x1, mods[1], ln_g[1], ln_b[1], tm=512)
    return x2, new_k, new_v, v_n, new_conv, s_new


def kernel(x_prompt, x_sample, cache_a_k, cache_a_v, state_c_conv, state_c_s, c_prompt, c_sample, w_ada, b_ada,
           ln_g, ln_b, w_in_ab, rel_table, ln_v_g, ln_v_b, w_s, b_s, w_out_ab, w_in_dn, w_conv, a_log, dt_bias,
           o_norm_g, w_out_dn):
    bp = c_prompt.shape[0]
    bs = c_sample.shape[0]
    c_all = jnp.concatenate([c_prompt, c_sample], axis=0)
    c_all = jnp.pad(c_all, ((0, -(bp + bs) % 16), (0, 0)))
    ada = _ada(c_all, w_ada, b_ada)
    mods_p = [ada[l, :bp].reshape(bp, 3, D_MODEL) for l in range(DEPTH)]
    mods_s = [ada[l, bp:bp + bs].reshape(bs, 3, D_MODEL) for l in range(DEPTH)]

    extra = w_in_dn[:, C_MAIN:]
    w_extra = jnp.pad(extra, ((0, 0), (0, LANES - extra.shape[1]))).astype(BF16)
    ad_rows = jnp.zeros((2, LANES), F32)
    ad_rows = ad_rows.at[0, C_V_HEADS:2 * C_V_HEADS].set(a_log).at[1, C_V_HEADS:2 * C_V_HEADS].set(dt_bias)
    wts = (w_in_ab.astype(BF16), _band_bias_blocks(rel_table, 256), rel_table, ln_v_g, ln_v_b, w_s, b_s,
           w_out_ab.astype(BF16), w_in_dn[:, :C_MAIN].astype(BF16), w_extra, w_conv, ad_rows, o_norm_g,
           w_out_dn.astype(BF16), ln_g, ln_b)

    y_p, p_a_k, p_a_v, _, p_c_conv, p_c_s = _trunk(x_prompt, mods_p, wts, None, None, None, None, prompt=True)
    y_s, s_a_k, s_a_v, s_b_v, s_c_conv, s_c_s = _trunk(x_sample, mods_s, wts, cache_a_k, cache_a_v, state_c_conv,
                                                       state_c_s, prompt=False)
    s_b_v = s_b_v.reshape(bs, -1, B_GROUPS, B_GROUP_DIM)
    return (y_p, y_s, p_a_k, p_a_v, p_c_conv, p_c_s, s_a_k, s_a_v, s_b_v, s_c_conv, s_c_s)
```

```python
import functools

import jax
import jax.numpy as jnp
from jax import lax
from jax.experimental import pallas as pl
from jax.experimental.pallas import tpu as pltpu

F32 = jnp.float32
BF16 = jnp.bfloat16

D_MODEL = 2048
DEPTH = 2
CHUNK = 64
A_HEADS = 16
A_HEAD_DIM = 64
A_WIDTH = A_HEADS * A_HEAD_DIM
A_BAND = 8
REL_MAX = 128
B_GROUPS = 8
B_GROUP_DIM = 128
B_WIDTH = B_GROUPS * B_GROUP_DIM
B_CHUNK = 128
C_QK_HEADS = 16
C_V_HEADS = 32
C_HEAD_DIM = 128
C_QK_WIDTH = C_QK_HEADS * C_HEAD_DIM
C_V_WIDTH = C_V_HEADS * C_HEAD_DIM
C_CONV = 4
C_CONV_CH = 2 * C_QK_WIDTH + C_V_WIDTH
C_MAIN = C_CONV_CH + C_V_WIDTH
DN_ALPHA = (2 * DEPTH) ** 0.25
LN_EPS = 1e-5
NORM_EPS = 1e-6
NEG_BIG = -1e30

LANES = 128
HALO_ROWS = 8
VMEM_LIMIT = 56 * 1024 * 1024


def _cparams(sem):
    return pltpu.CompilerParams(dimension_semantics=sem, vmem_limit_bytes=VMEM_LIMIT)


def _silu(x):
    return x * jax.nn.sigmoid(x)


def _gelu(x):
    return 0.5 * x * (1.0 + lax.erf(x * (2.0 ** -0.5)))


def _layer_norm(x, g, b):
    mu = jnp.mean(x, axis=-1, keepdims=True)
    xc = x - mu
    var = jnp.mean(xc * xc, axis=-1, keepdims=True)
    return xc * lax.rsqrt(var + LN_EPS) * g + b


def _ada_kernel(c_ref, w_ref, b_ref, o_ref):
    c = c_ref[...].astype(BF16)
    w = w_ref[0].astype(BF16)
    o_ref[0] = jnp.dot(c, w, preferred_element_type=F32) + b_ref[0]


def _ada(c_all, w_ada, b_ada):
    rows, d = c_all.shape
    depth, _, n = w_ada.shape
    tn = 768
    return pl.pallas_call(
        _ada_kernel,
        grid=(depth, n // tn),
        in_specs=[
            pl.BlockSpec((rows, d), lambda l, j: (0, 0)),
            pl.BlockSpec((1, d, tn), lambda l, j: (l, 0, j)),
            pl.BlockSpec((1, 1, tn), lambda l, j: (l, 0, j)),
        ],
        out_specs=pl.BlockSpec((1, rows, tn), lambda l, j: (l, 0, j)),
        out_shape=jax.ShapeDtypeStruct((depth, rows, n), F32),
        compiler_params=_cparams(("parallel", "parallel")),
        name="ada",
    )(c_all, w_ada, b_ada.reshape(depth, 1, n))


def _inproj_kernel(x_ref, mod_ref, w_ref, *rest, has_extra):
    if has_extra:
        we_ref, o_ref, oe_ref, h_scr = rest
    else:
        o_ref, h_scr = rest
    j = pl.program_id(2)

    @pl.when(j == 0)
    def _():
        h = x_ref[0] * (1.0 + mod_ref[0, 1:2, :]) + mod_ref[0, 0:1, :]
        hb = h.astype(BF16)
        h_scr[...] = hb
        if has_extra:
            oe_ref[0] = jnp.dot(hb, we_ref[...], preferred_element_type=F32)

    o_ref[0] = jnp.dot(h_scr[...], w_ref[...], preferred_element_type=F32).astype(o_ref.dtype)


def _inproj(x, mod, w, w_extra=None, *, tm, tn=512):
    b, t, d = x.shape
    n = w.shape[1]
    tm = min(tm, t)
    has_extra = w_extra is not None
    in_specs = [
        pl.BlockSpec((1, tm, d), lambda bb, i, j: (bb, i, 0)),
        pl.BlockSpec((1, 3, d), lambda bb, i, j: (bb, 0, 0)),
        pl.BlockSpec((d, tn), lambda bb, i, j: (0, j)),
    ]
    out_specs = [pl.BlockSpec((1, tm, tn), lambda bb, i, j: (bb, i, j))]
    out_shape = [jax.ShapeDtypeStruct((b, t, n), BF16)]
    args = [x, mod, w]
    if has_extra:
        ne = w_extra.shape[1]
        in_specs.append(pl.BlockSpec((d, ne), lambda bb, i, j: (0, 0)))
        out_specs.append(pl.BlockSpec((1, tm, ne), lambda bb, i, j: (bb, i, 0)))
        out_shape.append(jax.ShapeDtypeStruct((b, t, ne), F32))
        args.append(w_extra)
    res = pl.pallas_call(
        functools.partial(_inproj_kernel, has_extra=has_extra),
        grid=(b, t // tm, n // tn),
        in_specs=in_specs,
        out_specs=out_specs,
        out_shape=out_shape,
        scratch_shapes=[pltpu.VMEM((tm, d), BF16)],
        compiler_params=_cparams(("parallel", "parallel", "arbitrary")),
        name="inproj",
    )(*args)
    return res if has_extra else res[0]


def _attn_kernel(q_ref, k0_ref, k1_ref, k2_ref, v0_ref, v1_ref, v2_ref, z_ref, bias_ref, o_ref, *, tq):
    i = pl.program_id(2)
    q = q_ref[0]
    k = jnp.concatenate([k0_ref[0], k1_ref[0], k2_ref[0]], axis=0)
    v = jnp.concatenate([v0_ref[0], v1_ref[0], v2_ref[0]], axis=0)
    col = lax.broadcasted_iota(jnp.int32, (1, 3 * tq), 1)
    missing = col < (2 - i) * tq
    outs = []
    for hh in range(2):
        sl = slice(hh * A_HEAD_DIM, (hh + 1) * A_HEAD_DIM)
        s = lax.dot_general(q[:, sl], k[:, sl], (((1,), (1,)), ((), ())), preferred_element_type=F32)
        s = s * (A_HEAD_DIM ** -0.5) + bias_ref[hh]
        s = jnp.where(missing, NEG_BIG, s)
        m = jnp.max(s, axis=-1, keepdims=True)
        p = jnp.exp(s - m)
        l = jnp.sum(p, axis=-1, keepdims=True)
        o = jnp.dot(p.astype(BF16), v[:, sl], preferred_element_type=F32)
        outs.append(o / l)
    o = jnp.concatenate(outs, axis=1)
    o_ref[0] = (o * _silu(z_ref[0].astype(F32))).astype(o_ref.dtype)


def _band_bias_blocks(rel_table, tq):
    band_len = (A_BAND + 1) * CHUNK
    dist = jnp.arange(CHUNK)[:, None] + A_BAND * CHUNK - jnp.arange(band_len)[None, :]
    cb = rel_table[:, jnp.clip(dist, -REL_MAX, REL_MAX) + REL_MAX].astype(F32)
    nq = tq // CHUNK
    lead = 2 * tq - A_BAND * CHUNK
    rows = []
    for r in range(nq):
        left = lead + r * CHUNK
        right = 3 * tq - left - band_len
        rows.append(jnp.pad(cb, ((0, 0), (0, 0), (left, right)), constant_values=NEG_BIG))
    return jnp.concatenate(rows, axis=1)


def _attn_prompt(p0, bias, *, tq=256):
    b, t, _ = p0.shape
    npair = A_HEADS // 2
    qo, ko, vo, zo = 0, A_WIDTH // LANES, 2 * A_WIDTH // LANES, 3 * A_WIDTH // LANES

    def blk(off, back):
        return pl.BlockSpec((1, tq, LANES), lambda hp, bb, i: (bb, jnp.maximum(i - back, 0), off + hp))

    return pl.pallas_call(
        functools.partial(_attn_kernel, tq=tq),
        grid=(npair, b, t // tq),
        in_specs=[blk(qo, 0), blk(ko, 2), blk(ko, 1), blk(ko, 0), blk(vo, 2), blk(vo, 1), blk(vo, 0), blk(zo, 0),
                  pl.BlockSpec((2, tq, 3 * tq), lambda hp, bb, i: (hp, 0, 0))],
        out_specs=pl.BlockSpec((1, tq, LANES), lambda hp, bb, i: (bb, i, hp)),
        out_shape=jax.ShapeDtypeStruct((b, t, A_WIDTH), BF16),
        compiler_params=_cparams(("parallel", "parallel", "arbitrary")),
        name="attn_prompt",
    )(p0, p0, p0, p0, p0, p0, p0, p0, bias)


def _attn_sample_kernel(q_ref, k_ref, v_ref, z_ref, ck_ref, cv_ref, bias_ref, o_ref, *, pad_rows):
    q = q_ref[0]
    zpad = jnp.zeros((pad_rows, A_WIDTH), BF16)
    k = jnp.concatenate([ck_ref[0], k_ref[0], zpad], axis=0)
    v = jnp.concatenate([cv_ref[0], v_ref[0], zpad], axis=0)
    outs = []
    for h in range(A_HEADS):
        sl = slice(h * A_HEAD_DIM, (h + 1) * A_HEAD_DIM)
        s = lax.dot_general(q[:, sl], k[:, sl], (((1,), (1,)), ((), ())), preferred_element_type=F32)
        s = s * (A_HEAD_DIM ** -0.5) + bias_ref[h]
        m = jnp.max(s, axis=-1, keepdims=True)
        p = jnp.exp(s - m)
        l = jnp.sum(p, axis=-1, keepdims=True)
        o = jnp.dot(p.astype(BF16), v[:, sl], preferred_element_type=F32)
        outs.append(o / l)
    o = jnp.concatenate(outs, axis=1)
    o_ref[0] = (o * _silu(z_ref[0].astype(F32))).astype(o_ref.dtype)


def _attn_sample(p0, cache_k, cache_v, rel_table):
    b, t, _ = p0.shape
    n_cache = cache_k.shape[1]
    n_keys = n_cache + t
    n_pad = -n_keys % LANES
    dist = jnp.arange(t)[:, None] + n_cache - jnp.arange(n_keys)[None, :]
    bias = rel_table[:, jnp.clip(dist, -REL_MAX, REL_MAX) + REL_MAX].astype(F32)
    bias = jnp.pad(bias, ((0, 0), (0, 0), (0, n_pad)), constant_values=NEG_BIG)
    ck = cache_k.reshape(b, n_cache, A_WIDTH).astype(BF16)
    cv = cache_v.reshape(b, n_cache, A_WIDTH).astype(BF16)

    def col(j):
        return pl.BlockSpec((1, t, A_WIDTH), lambda bb: (bb, 0, j))

    cache_spec = pl.BlockSpec((1, n_cache, A_WIDTH), lambda bb: (bb, 0, 0))
    return pl.pallas_call(
        functools.partial(_attn_sample_kernel, pad_rows=n_pad),
        grid=(b,),
        in_specs=[col(0), col(1), col(2), col(3), cache_spec, cache_spec,
                  pl.BlockSpec((A_HEADS, t, n_keys + n_pad), lambda bb: (0, 0, 0))],
        out_specs=pl.BlockSpec((1, t, A_WIDTH), lambda bb: (bb, 0, 0)),
        out_shape=jax.ShapeDtypeStruct((b, t, A_WIDTH), BF16),
        compiler_params=_cparams(("parallel",)),
        name="attn_sample",
    )(p0, p0, p0, p0, ck, cv, bias)


def _gmlp_kernel(u_ref, v_ref, z_ref, ws_ref, bst_ref, g_ref, b_ref, o_ref, *vn_out, n_mix, tb):
    vn = _layer_norm(_gelu(v_ref[0].astype(F32)), g_ref[...], b_ref[...])
    if vn_out:
        vn_out[0][0] = vn
    z = z_ref[0].astype(F32)
    gate = _gelu(u_ref[0].astype(F32)) * _silu(z)
    row = lax.broadcasted_iota(jnp.int32, (B_CHUNK, B_CHUNK), 0)
    colm = lax.broadcasted_iota(jnp.int32, (B_CHUNK, B_CHUNK), 1)
    keep = (colm <= row) & (row < n_mix)
    rows = min(tb, B_CHUNK)
    for g in range(B_GROUPS):
        gs = slice(g * B_GROUP_DIM, (g + 1) * B_GROUP_DIM)
        w = jnp.where(keep, ws_ref[g], 0.0).astype(BF16)
        bias = bst_ref[:, g:g + 1]
        for c in range(max(tb // B_CHUNK, 1)):
            rs = slice(c * B_CHUNK, c * B_CHUNK + rows)
            vg = vn[rs, gs].astype(BF16)
            if rows < B_CHUNK:
                vg = jnp.concatenate([vg, jnp.zeros((B_CHUNK - rows, B_GROUP_DIM), BF16)], axis=0)
            mix = jnp.dot(w, vg, preferred_element_type=F32) + bias
            o_ref[0, rs, gs] = (gate[rs, gs] * mix[:rows]).astype(o_ref.dtype)


def _gmlp(p0, w_s, b_s, ln_v_g, ln_v_b, *, tb, want_vn):
    b, t, _ = p0.shape
    tb = min(tb, t)
    n_mix = min(t, B_CHUNK)
    base = 4 * A_WIDTH // B_WIDTH

    def col(j):
        return pl.BlockSpec((1, tb, B_WIDTH), lambda bb, i: (bb, i, base + j))

    out_specs = [pl.BlockSpec((1, tb, B_WIDTH), lambda bb, i: (bb, i, 0))]
    out_shape = [jax.ShapeDtypeStruct((b, t, B_WIDTH), BF16)]
    if want_vn:
        out_specs.append(pl.BlockSpec((1, tb, B_WIDTH), lambda bb, i: (bb, i, 0)))
        out_shape.append(jax.ShapeDtypeStruct((b, t, B_WIDTH), F32))
    res = pl.pallas_call(
        functools.partial(_gmlp_kernel, n_mix=n_mix, tb=tb),
        grid=(b, t // tb),
        in_specs=[col(0), col(1), col(2),
                  pl.BlockSpec((B_GROUPS, B_CHUNK, B_CHUNK), lambda bb, i: (0, 0, 0)),
                  pl.BlockSpec((B_CHUNK, B_GROUPS), lambda bb, i: (0, 0)),
                  pl.BlockSpec((1, B_WIDTH), lambda bb, i: (0, 0)),
                  pl.BlockSpec((1, B_WIDTH), lambda bb, i: (0, 0))],
        out_specs=out_specs,
        out_shape=out_shape,
        compiler_params=_cparams(("parallel", "parallel")),
        name="gmlp",
    )(p0, p0, p0, w_s, b_s.T, ln_v_g.reshape(1, -1), ln_v_b.reshape(1, -1))
    return res if want_vn else res[0]


def _outproj_kernel(*refs, widths):
    o_refs = refs[:len(widths)]
    w_ref, x_ref, mod_ref, g_ref, b_ref, out_ref = refs[len(widths):]
    y = None
    off = 0
    for o_ref, kw in zip(o_refs, widths):
        part = jnp.dot(o_ref[0], w_ref[off:off + kw, :], preferred_element_type=F32)
        y = part if y is None else y + part
        off += kw
    r = DN_ALPHA * x_ref[0] + (1.0 + mod_ref[0, 2:3, :]) * y
    out_ref[0] = _layer_norm(r, g_ref[...], b_ref[...])


def _outproj(os_, w, x, mod, ln_g, ln_b, *, tm):
    b, t, d = x.shape
    tm = min(tm, t)
    widths = tuple(o.shape[-1] for o in os_)
    ktot = sum(widths)
    in_specs = [pl.BlockSpec((1, tm, kw), lambda bb, i: (bb, i, 0)) for kw in widths]
    in_specs += [
        pl.BlockSpec((ktot, d), lambda bb, i: (0, 0), pipeline_mode=pl.Buffered(1)),
        pl.BlockSpec((1, tm, d), lambda bb, i: (bb, i, 0)),
        pl.BlockSpec((1, 3, d), lambda bb, i: (bb, 0, 0)),
        pl.BlockSpec((1, d), lambda bb, i: (0, 0)),
        pl.BlockSpec((1, d), lambda bb, i: (0, 0)),
    ]
    return pl.pallas_call(
        functools.partial(_outproj_kernel, widths=widths),
        grid=(b, t // tm),
        in_specs=in_specs,
        out_specs=pl.BlockSpec((1, tm, d), lambda bb, i: (bb, i, 0)),
        out_shape=jax.ShapeDtypeStruct((b, t, d), F32),
        compiler_params=_cparams(("parallel", "parallel")),
        name="outproj",
    )(*os_, w, x, mod, ln_g.reshape(1, d), ln_b.reshape(1, d))


def _delta_kernel(q_ref, k_ref, v_ref, z_ref, ba_ref, hq0_ref, hk0_ref, hv0_ref, wq_ref, wk_ref, wv_ref,
                  ad_ref, gn_ref, s0_ref, o_ref, s_ref, hq_scr, hk_scr, hv_scr, gt_scr,
                  *, heads, nblk, blk, t_valid):
    hg = pl.program_id(1)
    n = pl.program_id(2)
    tb = nblk * blk

    @pl.when(n == 0)
    def _():
        hq_scr[...] = hq0_ref[0]
        hk_scr[...] = hk0_ref[0]
        hv_scr[...] = hv0_ref[0]
        s_ref[...] = s0_ref[...]

    def conv_silu(raw_ref, halo_scr, w_ref):
        x = raw_ref[0].astype(F32)
        ext = jnp.concatenate([halo_scr[...], x], axis=0)
        y = w_ref[C_CONV - 1:C_CONV, :] * x
        for tap in range(C_CONV - 1):
            lo = HALO_ROWS - (C_CONV - 1) + tap
            y = y + w_ref[tap:tap + 1, :] * ext[lo:lo + tb]
        halo_scr[...] = x[tb - HALO_ROWS:tb]
        return _silu(y)

    qc = conv_silu(q_ref, hq_scr, wq_ref)
    kc = conv_silu(k_ref, hk_scr, wk_ref)
    vc = conv_silu(v_ref, hv_scr, wv_ref)

    def l2n(x):
        return x * lax.rsqrt(jnp.sum(x * x, axis=-1, keepdims=True) + NORM_EPS)

    qn, kn = [], []
    for j in range(heads // 2):
        hs = slice(j * C_HEAD_DIM, (j + 1) * C_HEAD_DIM)
        qn.append(l2n(qc[:, hs]) * (C_HEAD_DIM ** -0.5))
        kn.append(l2n(kc[:, hs]))

    ba = ba_ref[0]
    beta_all = jax.nn.sigmoid(ba)
    xa = ba + ad_ref[1:2, :]
    softplus = jnp.maximum(xa, 0.0) + jnp.log1p(jnp.exp(-jnp.abs(xa)))
    la_all = -jnp.exp(ad_ref[0:1, :]) * softplus
    if t_valid < tb:
        rowv = lax.broadcasted_iota(jnp.int32, (tb, LANES), 0) < t_valid
        beta_all = jnp.where(rowv, beta_all, 0.0)
        la_all = jnp.where(rowv, la_all, 0.0)

    lane = lax.broadcasted_iota(jnp.int32, (blk, LANES), 1)
    ri = lax.broadcasted_iota(jnp.int32, (blk, blk), 0)
    ci = lax.broadcasted_iota(jnp.int32, (blk, blk), 1)
    incl = ci <= ri
    strict = ci < ri
    ltri = incl.astype(F32)
    eye = (ci == ri).astype(F32)
    gn = gn_ref[...]

    items = [(nb, hh) for nb in range(nblk) for hh in range(heads)]
    rows = [slice(nb * blk, (nb + 1) * blk) for nb in range(nblk)]
    g_blk = []
    for nb in range(nblk):
        g_all = jnp.dot(ltri, la_all[rows[nb]], preferred_element_type=F32, precision=lax.Precision.HIGHEST)
        gt_scr[nb] = g_all.T
        g_blk.append(g_all)
    beta_c, g_c, g_last, dec, eg = {}, {}, {}, {}, {}
    for it in items:
        nb, hh = it
        hidx = hg * heads + hh
        beta_c[it] = jnp.sum(jnp.where(lane == hidx, beta_all[rows[nb]], 0.0), axis=1, keepdims=True)
        g_c[it] = jnp.sum(jnp.where(lane == C_V_HEADS + hidx, g_blk[nb], 0.0), axis=1, keepdims=True)
        g_r = gt_scr[nb, pl.ds(C_V_HEADS + hidx, 1), :]
        g_last[it] = g_c[it][blk - 1:blk, :]
        dec[it] = jnp.where(incl, jnp.exp(g_c[it] - g_r), 0.0)
        eg[it] = jnp.exp(g_c[it])
    kb, qk, pw, tinv = {}, {}, {}, {}
    for nb in range(nblk):
        for j in range(heads // 2):
            h0, h1 = (nb, 2 * j), (nb, 2 * j + 1)
            k = kn[j][rows[nb]]
            kb[h0] = k * beta_c[h0]
            kb[h1] = k * beta_c[h1]
            lhs = jnp.concatenate([qn[j][rows[nb]], kb[h0], kb[h1]], axis=0).astype(BF16)
            qa = lax.dot_general(lhs, k.astype(BF16), (((1,), (1,)), ((), ())), preferred_element_type=F32)
            for m, it in enumerate((h0, h1)):
                qk[it] = (qa[:blk] * dec[it]).astype(BF16)
                pw[it] = jnp.where(strict, qa[(m + 1) * blk:(m + 2) * blk] * dec[it], 0.0)
                tinv[it] = eye - pw[it]
    span = 2
    while span < blk:
        for it in items:
            pwb = pw[it].astype(BF16)
            pw[it] = jnp.dot(pwb, pwb, preferred_element_type=F32)
        for it in items:
            tinv[it] = tinv[it] + jnp.dot(tinv[it].astype(BF16), pw[it].astype(BF16), preferred_element_type=F32)
        span *= 2
    wq, u = {}, {}
    for it in items:
        nb, hh = it
        v = vc[rows[nb], hh * C_HEAD_DIM:(hh + 1) * C_HEAD_DIM]
        rhs = jnp.concatenate([kb[it] * eg[it], v * beta_c[it]], axis=1).astype(BF16)
        wu = jnp.dot(tinv[it].astype(BF16), rhs, preferred_element_type=F32)
        qd = qn[hh // 2][rows[nb]] * eg[it]
        wq[it] = jnp.concatenate([wu[:, :C_HEAD_DIM], qd], axis=0).astype(BF16)
        u[it] = wu[:, C_HEAD_DIM:]
    s = [s_ref[0, hh] for hh in range(heads)]
    for nb in range(nblk):
        ws_qs = [jnp.dot(wq[(nb, hh)], s[hh].astype(BF16), preferred_element_type=F32) for hh in range(heads)]
        vnb = [(u[(nb, hh)] - ws_qs[hh][:blk]).astype(BF16) for hh in range(heads)]
        for hh in range(heads):
            it = (nb, hh)
            k_dec = (kn[hh // 2][rows[nb]] * jnp.exp(g_last[it] - g_c[it])).astype(BF16)
            s[hh] = s[hh] * jnp.exp(g_last[it]) + lax.dot_general(
                k_dec, vnb[hh], (((0,), (0,)), ((), ())), preferred_element_type=F32)
        for hh in range(heads):
            o = ws_qs[hh][blk:] + jnp.dot(qk[(nb, hh)], vnb[hh], preferred_element_type=F32)
            on = o * lax.rsqrt(jnp.mean(o * o, axis=-1, keepdims=True) + NORM_EPS) * gn
            hs = slice(hh * C_HEAD_DIM, (hh + 1) * C_HEAD_DIM)
            o_ref[0, rows[nb], hs] = (on * _silu(z_ref[0, rows[nb], hs].astype(F32))).astype(o_ref.dtype)
    for hh in range(heads):
        s_ref[0, hh] = s[hh]


def _delta(p1, ba, conv_left8, s0, w_conv, ad_rows, o_norm_g, *, heads, nblk, t_valid):
    b, t, _ = p1.shape
    blk = CHUNK
    tb = nblk * blk
    cq = heads // 2 * C_HEAD_DIM
    cv = heads * C_HEAD_DIM
    ko, vo, zo = C_QK_WIDTH // cq, 2 * C_QK_WIDTH // cv, C_CONV_CH // cv

    def tok(width, off):
        return pl.BlockSpec((1, tb, width), lambda bb, hg, n: (bb, n, off + hg))

    def halo(width, off):
        return pl.BlockSpec((1, HALO_ROWS, width), lambda bb, hg, n: (bb, 0, off + hg))

    def wc(width, off):
        return pl.BlockSpec((C_CONV, width), lambda bb, hg, n: (0, off + hg))

    state_spec = pl.BlockSpec((1, heads, C_HEAD_DIM, C_HEAD_DIM), lambda bb, hg, n: (bb, hg, 0, 0))
    return pl.pallas_call(
        functools.partial(_delta_kernel, heads=heads, nblk=nblk, blk=blk, t_valid=t_valid),
        grid=(b, C_V_HEADS // heads, t // tb),
        in_specs=[tok(cq, 0), tok(cq, ko), tok(cv, vo), tok(cv, zo),
                  pl.BlockSpec((1, tb, LANES), lambda bb, hg, n: (bb, n, 0)),
                  halo(cq, 0), halo(cq, ko), halo(cv, vo),
                  wc(cq, 0), wc(cq, ko), wc(cv, vo),
                  pl.BlockSpec((2, LANES), lambda bb, hg, n: (0, 0)),
                  pl.BlockSpec((1, C_HEAD_DIM), lambda bb, hg, n: (0, 0)),
                  state_spec],
        out_specs=[pl.BlockSpec((1, tb, cv), lambda bb, hg, n: (bb, n, hg)), state_spec],
        out_shape=[jax.ShapeDtypeStruct((b, t, C_V_WIDTH), BF16),
                   jax.ShapeDtypeStruct((b, C_V_HEADS, C_HEAD_DIM, C_HEAD_DIM), F32)],
        scratch_shapes=[pltpu.VMEM((HALO_ROWS, cq), F32), pltpu.VMEM((HALO_ROWS, cq), F32),
                        pltpu.VMEM((HALO_ROWS, cv), F32), pltpu.VMEM((nblk, LANES, blk), F32)],
        compiler_params=_cparams(("parallel", "parallel", "arbitrary")),
        name="delta",
    )(p1, p1, p1, p1, ba, conv_left8, conv_left8, conv_left8, w_conv, w_conv, w_conv,
      ad_rows, o_norm_g.reshape(1, C_HEAD_DIM), s0)


def _trunk(x, mods, wts, cache_k, cache_v, conv_left, s0, *, prompt):
    (w_in_ab, bias_blocks, rel_table, ln_v_g, ln_v_b, w_s, b_s, w_out_ab, w_in_dn, w_extra, w_conv, ad_rows,
     o_norm_g, w_out_dn, ln_g, ln_b) = wts
    b, t, _ = x.shape
    p0 = _inproj(x, mods[0], w_in_ab, tm=1024)
    if prompt:
        o_a = _attn_prompt(p0, bias_blocks)
        o_b = _gmlp(p0, w_s, b_s, ln_v_g, ln_v_b, tb=256, want_vn=False)
        v_n = None
        keep = min(A_BAND * CHUNK, t)
    else:
        o_a = _attn_sample(p0, cache_k, cache_v, rel_table)
        o_b, v_n = _gmlp(p0, w_s, b_s, ln_v_g, ln_v_b, tb=256, want_vn=True)
        keep = t
    new_k = p0[:, t - keep:, A_WIDTH:2 * A_WIDTH].astype(F32).reshape(b, keep, A_HEADS, A_HEAD_DIM)
    new_v = p0[:, t - keep:, 2 * A_WIDTH:3 * A_WIDTH].astype(F32).reshape(b, keep, A_HEADS, A_HEAD_DIM)
    x1 = _outproj([o_a, o_b], w_out_ab, x, mods[0], ln_g[0], ln_b[0], tm=512)
    p1, ba = _inproj(x1, mods[1], w_in_dn, w_extra, tm=1024)
    new_conv = p1[:, t - (C_CONV - 1):, :C_CONV_CH].astype(F32)
    if conv_left is None:
        left8 = jnp.zeros((b, HALO_ROWS, C_CONV_CH), F32)
    else:
        left8 = jnp.pad(conv_left, ((0, 0), (HALO_ROWS - (C_CONV - 1), 0), (0, 0)))
    if s0 is None:
        s0 = jnp.zeros((b, C_V_HEADS, C_HEAD_DIM, C_HEAD_DIM), F32)
    t_pad = -t % CHUNK
    if t_pad:
        p1 = jnp.pad(p1, ((0, 0), (0, t_pad), (0, 0)))
        ba = jnp.pad(ba, ((0, 0), (0, t_pad), (0, 0)))
    o_c, s_new = _delta(p1, ba, left8, s0, w_conv, ad_rows, o_norm_g,
                        heads=8, nblk=2 if (t + t_pad) % (2 * CHUNK) == 0 else 1, t_valid=t)
    if t_pad:
        o_c = o_c[:, :t]
    x2 = _outproj([o_c], w_out_dn, x1, mods[1], ln_g[1], ln_b[1], tm=512)
    return x2, new_k, new_v, v_n, new_conv, s_new


def kernel(x_prompt, x_sample, cache_a_k, cache_a_v, state_c_conv, state_c_s, c_prompt, c_sample, w_ada, b_ada,
           ln_g, ln_b, w_in_ab, rel_table, ln_v_g, ln_v_b, w_s, b_s, w_out_ab, w_in_dn, w_conv, a_log, dt_bias,
           o_norm_g, w_out_dn):
    bp = c_prompt.shape[0]
    bs = c_sample.shape[0]
    c_all = jnp.concatenate([c_prompt, c_sample], axis=0)
    c_all = jnp.pad(c_all, ((0, -(bp + bs) % 16), (0, 0)))
    ada = _ada(c_all, w_ada, b_ada)
    mods_p = [ada[l, :bp].reshape(bp, 3, D_MODEL) for l in range(DEPTH)]
    mods_s = [ada[l, bp:bp + bs].reshape(bs, 3, D_MODEL) for l in range(DEPTH)]

    extra = w_in_dn[:, C_MAIN:]
    w_extra = jnp.pad(extra, ((0, 0), (0, LANES - extra.shape[1]))).astype(BF16)
    ad_rows = jnp.zeros((2, LANES), F32)
    ad_rows = ad_rows.at[0, C_V_HEADS:2 * C_V_HEADS].set(a_log).at[1, C_V_HEADS:2 * C_V_HEADS].set(dt_bias)
    wts = (w_in_ab.astype(BF16), _band_bias_blocks(rel_table, 256), rel_table, ln_v_g, ln_v_b, w_s, b_s,
           w_out_ab.astype(BF16), w_in_dn[:, :C_MAIN].astype(BF16), w_extra, w_conv, ad_rows, o_norm_g,
           w_out_dn.astype(BF16), ln_g, ln_b)

    y_p, p_a_k, p_a_v, _, p_c_conv, p_c_s = _trunk(x_prompt, mods_p, wts, None, None, None, None, prompt=True)
    y_s, s_a_k, s_a_v, s_b_v, s_c_conv, s_c_s = _trunk(x_sample, mods_s, wts, cache_a_k, cache_a_v, state_c_conv,
                                                       state_c_s, prompt=False)
    s_b_v = s_b_v.reshape(bs, -1, B_GROUPS, B_GROUP_DIM)
    return (y_p, y_s, p_a_k, p_a_v, p_c_conv, p_c_s, s_a_k, s_a_v, s_b_v, s_c_conv, s_c_s)
```

```python
import functools

import jax
import jax.numpy as jnp
from jax import lax
from jax.experimental import pallas as pl
from jax.experimental.pallas import tpu as pltpu

F32 = jnp.float32
BF16 = jnp.bfloat16

D_MODEL = 2048
DEPTH = 2
CHUNK = 64
A_HEADS = 16
A_HEAD_DIM = 64
A_WIDTH = A_HEADS * A_HEAD_DIM
A_BAND = 8
REL_MAX = 128
B_GROUPS = 8
B_GROUP_DIM = 128
B_WIDTH = B_GROUPS * B_GROUP_DIM
B_CHUNK = 128
C_QK_HEADS = 16
C_V_HEADS = 32
C_HEAD_DIM = 128
C_QK_WIDTH = C_QK_HEADS * C_HEAD_DIM
C_V_WIDTH = C_V_HEADS * C_HEAD_DIM
C_CONV = 4
C_CONV_CH = 2 * C_QK_WIDTH + C_V_WIDTH
C_MAIN = C_CONV_CH + C_V_WIDTH
DN_ALPHA = (2 * DEPTH) ** 0.25
LN_EPS = 1e-5
NORM_EPS = 1e-6
NEG_BIG = -1e30

LANES = 128
HALO_ROWS = 8
VMEM_LIMIT = 56 * 1024 * 1024


def _cparams(sem):
    return pltpu.CompilerParams(dimension_semantics=sem, vmem_limit_bytes=VMEM_LIMIT)


def _silu(x):
    return x * jax.nn.sigmoid(x)


def _gelu(x):
    return 0.5 * x * (1.0 + lax.erf(x * (2.0 ** -0.5)))


def _layer_norm(x, g, b):
    mu = jnp.mean(x, axis=-1, keepdims=True)
    xc = x - mu
    var = jnp.mean(xc * xc, axis=-1, keepdims=True)
    return xc * lax.rsqrt(var + LN_EPS) * g + b


def _ada_kernel(c_ref, w_ref, b_ref, o_ref):
    c = c_ref[...].astype(BF16)
    w = w_ref[0].astype(BF16)
    o_ref[0] = jnp.dot(c, w, preferred_element_type=F32) + b_ref[0]


def _ada(c_all, w_ada, b_ada):
    rows, d = c_all.shape
    depth, _, n = w_ada.shape
    tn = 768
    return pl.pallas_call(
        _ada_kernel,
        grid=(depth, n // tn),
        in_specs=[
            pl.BlockSpec((rows, d), lambda l, j: (0, 0)),
            pl.BlockSpec((1, d, tn), lambda l, j: (l, 0, j)),
            pl.BlockSpec((1, 1, tn), lambda l, j: (l, 0, j)),
        ],
        out_specs=pl.BlockSpec((1, rows, tn), lambda l, j: (l, 0, j)),
        out_shape=jax.ShapeDtypeStruct((depth, rows, n), F32),
        compiler_params=_cparams(("parallel", "parallel")),
        name="ada",
    )(c_all, w_ada, b_ada.reshape(depth, 1, n))


def _inproj_kernel(*refs, bt, tm, tn, cm, cn, dn):
    if dn:
        x_ref, mod_ref, w_ref, we_ref, wc_ref, cl_ref, o_ref, oe_ref, tail_ref, h_scr, halo_scr = refs
    else:
        x_ref, mod_ref, w_ref, o_ref, h_scr = refs
    i = pl.program_id(1)
    j = pl.program_id(2)
    rows = bt * tm

    @pl.when(j == 0)
    def _():
        h = x_ref[...] * (1.0 + mod_ref[:, 1:2, :]) + mod_ref[:, 0:1, :]
        hb = h.reshape(rows, h.shape[-1]).astype(BF16)
        h_scr[...] = hb
        if dn:
            oe = jnp.dot(hb, we_ref[...], preferred_element_type=F32)
            oe_ref[...] = oe.reshape(bt, tm, oe.shape[-1])

    def plain():
        acc = jnp.dot(h_scr[...], w_ref[...], preferred_element_type=F32)
        o_ref[...] = acc.reshape(bt, tm, tn).astype(o_ref.dtype)

    if not dn:
        plain()
        return

    n_conv = C_CONV_CH // tn

    def conv_tiles():
        is_qk = j < (2 * C_QK_WIDTH) // tn
        qscale = jnp.where(j < C_QK_WIDTH // tn, C_HEAD_DIM ** -0.5, 1.0)
        row8 = lax.broadcasted_iota(jnp.int32, (HALO_ROWS, cn), 0)
        ncp = tn // cn

        def cols(nt):
            return slice(nt * cn, (nt + 1) * cn)

        def epilogue(seg, prev, nt):
            taps = wc_ref[:, cols(nt)]
            y = taps[C_CONV - 1:C_CONV] * seg
            for delay in range(1, C_CONV):
                rolled = pltpu.roll(seg, delay, 0)
                first = jnp.where(row8 < delay, pltpu.roll(prev, delay, 0), rolled[:HALO_ROWS])
                shifted = jnp.concatenate([first, rolled[HALO_ROWS:]], axis=0)
                y = y + taps[C_CONV - 1 - delay:C_CONV - delay] * shifted
            s = _silu(y)
            outs = []
            for hd in range(cn // C_HEAD_DIM):
                sg = s[:, hd * C_HEAD_DIM:(hd + 1) * C_HEAD_DIM]
                f = lax.rsqrt(jnp.sum(sg * sg, axis=-1, keepdims=True) + NORM_EPS) * qscale
                outs.append(sg * jnp.where(is_qk, f, 1.0))
            return jnp.concatenate(outs, axis=1).astype(o_ref.dtype)

        if bt == 1:
            @pl.when(i == 0)
            def _():
                halo_scr[j] = cl_ref[0]

            prev = {nt: halo_scr[j, :, cols(nt)] for nt in range(ncp)}
            pieces = [(c, nt) for c in range(tm // cm) for nt in range(ncp)]
            pending = None
            for p in range(len(pieces) + 1):
                issued = None
                if p < len(pieces):
                    c, nt = pieces[p]
                    issued = (c, nt, jnp.dot(h_scr[c * cm:(c + 1) * cm, :], w_ref[:, cols(nt)],
                                             preferred_element_type=F32))
                if pending is not None:
                    c, nt, a = pending
                    o_ref[0, c * cm:(c + 1) * cm, cols(nt)] = epilogue(a, prev[nt], nt)
                    prev[nt] = a[cm - HALO_ROWS:cm]
                pending = issued
            for nt in range(ncp):
                halo_scr[j, :, cols(nt)] = prev[nt]
                tail_ref[0, 0, :, cols(nt)] = prev[nt]
        else:
            for nt in range(ncp):
                acc = jnp.dot(h_scr[...], w_ref[:, cols(nt)], preferred_element_type=F32)
                for bi in range(bt):
                    seg = acc[bi * tm:(bi + 1) * tm]
                    o_ref[bi, :, cols(nt)] = epilogue(seg, cl_ref[bi, :, cols(nt)], nt)
                    tail_ref[bi, 0, :, cols(nt)] = seg[tm - HALO_ROWS:tm]

    pl.when(j < n_conv)(conv_tiles)
    pl.when(j >= n_conv)(plain)


def _inproj(x, mod, w, dn_args=None, *, tm, tn, cm, cn=256):
    b, t, d = x.shape
    n = w.shape[1]
    tm = min(tm, t)
    bt = b if t == tm and b * t <= 1024 else 1
    cm = min(cm, tm)
    dn = dn_args is not None
    in_specs = [
        pl.BlockSpec((bt, tm, d), lambda bb, i, j: (bb, i, 0)),
        pl.BlockSpec((bt, 3, d), lambda bb, i, j: (bb, 0, 0)),
        pl.BlockSpec((d, tn), lambda bb, i, j: (0, j)),
    ]
    out_specs = [pl.BlockSpec((bt, tm, tn), lambda bb, i, j: (bb, i, j))]
    out_shape = [jax.ShapeDtypeStruct((b, t, n), BF16)]
    scratch = [pltpu.VMEM((bt * tm, d), BF16)]
    args = [x, mod, w]
    if dn:
        w_extra, w_conv, conv_left8 = dn_args
        ne = w_extra.shape[1]
        last = C_CONV_CH // tn - 1

        def cj(j):
            return jnp.minimum(j, last)

        in_specs += [
            pl.BlockSpec((d, ne), lambda bb, i, j: (0, 0)),
            pl.BlockSpec((C_CONV, tn), lambda bb, i, j: (0, cj(j))),
            pl.BlockSpec((bt, HALO_ROWS, tn), lambda bb, i, j: (bb, 0, cj(j))),
        ]
        out_specs += [
            pl.BlockSpec((bt, tm, ne), lambda bb, i, j: (bb, i, 0)),
            pl.BlockSpec((bt, 1, HALO_ROWS, tn), lambda bb, i, j: (bb, i, 0, cj(j))),
        ]
        out_shape += [jax.ShapeDtypeStruct((b, t, ne), F32),
                      jax.ShapeDtypeStruct((b, t // tm, HALO_ROWS, C_CONV_CH), F32)]
        scratch.append(pltpu.VMEM((C_CONV_CH // tn, HALO_ROWS, tn), F32))
        args += [w_extra, w_conv, conv_left8]
    res = pl.pallas_call(
        functools.partial(_inproj_kernel, bt=bt, tm=tm, tn=tn, cm=cm, cn=min(cn, tn), dn=dn),
        grid=(b // bt, t // tm, n // tn),
        in_specs=in_specs,
        out_specs=out_specs,
        out_shape=out_shape,
        scratch_shapes=scratch,
        compiler_params=_cparams(("parallel", "arbitrary", "arbitrary")),
        name="inproj_dn" if dn else "inproj",
    )(*args)
    return res if dn else res[0]


ATTN_TQ = 256
ATTN_SUB = 128
ATTN_WIN = ATTN_SUB + A_BAND * CHUNK
ATTN_HEADS = 4


def _attn_kernel(q_ref, k0_ref, k1_ref, k2_ref, v0_ref, v1_ref, v2_ref, z_ref, bias_ref, o_ref, bias_scr):
    i = pl.program_id(2)
    nsub = ATTN_TQ // ATTN_SUB

    @pl.when(i <= 2)
    def _():
        for sb in range(nsub):
            col = lax.broadcasted_iota(jnp.int32, (1, ATTN_WIN), 1) + sb * ATTN_SUB
            missing = col < (2 - i) * ATTN_TQ
            for h in range(ATTN_HEADS):
                bias_scr[sb * ATTN_HEADS + h] = jnp.where(missing, NEG_BIG, bias_ref[h])

    q = q_ref[0] * (A_HEAD_DIM ** -0.5)
    k = jnp.concatenate([k0_ref[0], k1_ref[0], k2_ref[0]], axis=0)
    v = jnp.concatenate([v0_ref[0], v1_ref[0], v2_ref[0]], axis=0)
    ones = jnp.ones((k.shape[0], A_HEAD_DIM), BF16)
    items = [(h, sb) for h in range(ATTN_HEADS) for sb in range(nsub)]

    def head(h):
        return slice(h * A_HEAD_DIM, (h + 1) * A_HEAD_DIM)

    v_ext = [jnp.concatenate([v[:, head(h)], ones], axis=1) for h in range(ATTN_HEADS)]

    def scores(it):
        h, sb = it
        qs = q[sb * ATTN_SUB:(sb + 1) * ATTN_SUB, head(h)]
        ks = k[sb * ATTN_SUB:sb * ATTN_SUB + ATTN_WIN, head(h)]
        s = lax.dot_general(qs, ks, (((1,), (1,)), ((), ())), preferred_element_type=F32)
        return s + bias_scr[sb * ATTN_HEADS + h]

    def attend(it, s):
        h, sb = it
        p = jnp.exp(s - jnp.max(s, axis=-1, keepdims=True)).astype(BF16)
        ol = jnp.dot(p, v_ext[h][sb * ATTN_SUB:sb * ATTN_SUB + ATTN_WIN], preferred_element_type=F32)
        return ol[:, :A_HEAD_DIM] / ol[:, A_HEAD_DIM:A_HEAD_DIM + 1]

    outs, pending = {}, {}
    for n in range(len(items) + 1):
        if n < len(items):
            pending[n] = scores(items[n])
        if n >= 1:
            outs[items[n - 1]] = attend(items[n - 1], pending.pop(n - 1))
    o = jnp.concatenate(
        [jnp.concatenate([outs[(h, sb)] for sb in range(nsub)], axis=0) for h in range(ATTN_HEADS)], axis=1)
    o_ref[0] = (o * _silu(z_ref[0].astype(F32))).astype(o_ref.dtype)


def _band_bias_tile(rel_table):
    h = rel_table.shape[0]
    far = A_BAND * CHUNK
    n_diag = ATTN_WIN + ATTN_SUB - 1
    n_const = far + ATTN_SUB - 1 - REL_MAX + 1
    w = jnp.concatenate([jnp.broadcast_to(rel_table[:, 2 * REL_MAX:], (h, n_const)),
                         rel_table[:, 2 * REL_MAX - 1::-1][:, :n_diag - n_const]], axis=1).astype(F32)
    tiled =jnp.tile(jnp.pad(w, ((0, 0), (0, 1))), (1, ATTN_SUB))[:, :ATTN_SUB * n_diag]
    toep = tiled.reshape(h, ATTN_SUB, n_diag)[:, :, ATTN_SUB - 1:ATTN_SUB - 1 + ATTN_WIN]
    row = jnp.arange(ATTN_SUB)[:, None]
    colx = jnp.arange(ATTN_WIN)[None, :]
    rel = colx - (row // CHUNK) * CHUNK
    band = (rel >= 0) & (rel < (A_BAND + 1) * CHUNK)
    return jnp.where(band[None], toep, NEG_BIG)


def _attn_prompt(p0, bias):
    b, t, _ = p0.shape
    tq = ATTN_TQ
    width = ATTN_HEADS * A_HEAD_DIM
    per = A_WIDTH // width
    qo, ko, vo, zo = 0, per, 2 * per, 3 * per

    def blk(off, back):
        return pl.BlockSpec((1, tq, width), lambda hg, bb, i: (bb, jnp.maximum(i - back, 0), off + hg))

    return pl.pallas_call(
        _attn_kernel,
        grid=(per, b, t // tq),
        in_specs=[blk(qo, 0), blk(ko, 2), blk(ko, 1), blk(ko, 0), blk(vo, 2), blk(vo, 1), blk(vo, 0), blk(zo, 0),
                  pl.BlockSpec((ATTN_HEADS, ATTN_SUB, ATTN_WIN), lambda hg, bb, i: (hg, 0, 0))],
        out_specs=pl.BlockSpec((1, tq, width), lambda hg, bb, i: (bb, i, hg)),
        out_shape=jax.ShapeDtypeStruct((b, t, A_WIDTH), BF16),
        scratch_shapes=[pltpu.VMEM((ATTN_TQ // ATTN_SUB * ATTN_HEADS, ATTN_SUB, ATTN_WIN), F32)],
        compiler_params=_cparams(("parallel", "parallel", "arbitrary")),
        name="attn_prompt",
    )(p0, p0, p0, p0, p0, p0, p0, p0, bias)


def _attn_sample_kernel(q_ref, k_ref, v_ref, z_ref, ck_ref, cv_ref, bias_ref, o_ref, *, pad_rows):
    q = q_ref[0]
    zpad = jnp.zeros((pad_rows, A_WIDTH), BF16)
    k = jnp.concatenate([ck_ref[0], k_ref[0], zpad], axis=0)
    v = jnp.concatenate([cv_ref[0], v_ref[0], zpad], axis=0)
    outs = []
    for h in range(A_HEADS):
        sl = slice(h * A_HEAD_DIM, (h + 1) * A_HEAD_DIM)
        s = lax.dot_general(q[:, sl], k[:, sl], (((1,), (1,)), ((), ())), preferred_element_type=F32)
        s = s * (A_HEAD_DIM ** -0.5) + bias_ref[h]
        m = jnp.max(s, axis=-1, keepdims=True)
        p = jnp.exp(s - m)
        l = jnp.sum(p, axis=-1, keepdims=True)
        o = jnp.dot(p.astype(BF16), v[:, sl], preferred_element_type=F32)
        outs.append(o / l)
    o = jnp.concatenate(outs, axis=1)
    o_ref[0] = (o * _silu(z_ref[0].astype(F32))).astype(o_ref.dtype)


def _attn_sample(p0, cache_k, cache_v, rel_table):
    b, t, _ = p0.shape
    n_cache = cache_k.shape[1]
    n_keys = n_cache + t
    n_pad = -n_keys % LANES
    dist = jnp.arange(t)[:, None] + n_cache - jnp.arange(n_keys)[None, :]
    bias = rel_table[:, jnp.clip(dist, -REL_MAX, REL_MAX) + REL_MAX].astype(F32)
    bias = jnp.pad(bias, ((0, 0), (0, 0), (0, n_pad)), constant_values=NEG_BIG)
    ck = cache_k.reshape(b, n_cache, A_WIDTH).astype(BF16)
    cv = cache_v.reshape(b, n_cache, A_WIDTH).astype(BF16)

    def col(j):
        return pl.BlockSpec((1, t, A_WIDTH), lambda bb: (bb, 0, j))

    cache_spec = pl.BlockSpec((1, n_cache, A_WIDTH), lambda bb: (bb, 0, 0))
    return pl.pallas_call(
        functools.partial(_attn_sample_kernel, pad_rows=n_pad),
        grid=(b,),
        in_specs=[col(0), col(1), col(2), col(3), cache_spec, cache_spec,
                  pl.BlockSpec((A_HEADS, t, n_keys + n_pad), lambda bb: (0, 0, 0))],
        out_specs=pl.BlockSpec((1, t, A_WIDTH), lambda bb: (bb, 0, 0)),
        out_shape=jax.ShapeDtypeStruct((b, t, A_WIDTH), BF16),
        compiler_params=_cparams(("parallel",)),
        name="attn_sample",
    )(p0, p0, p0, p0, ck, cv, bias)


def _gmlp_kernel(u_ref, v_ref, z_ref, ws_ref, bst_ref, g_ref, b_ref, o_ref, *vn_out, n_mix, tb):
    vn = _layer_norm(_gelu(v_ref[0].astype(F32)), g_ref[...], b_ref[...])
    if vn_out:
        vn_out[0][0] = vn
    z = z_ref[0].astype(F32)
    gate = _gelu(u_ref[0].astype(F32)) * _silu(z)
    row = lax.broadcasted_iota(jnp.int32, (B_CHUNK, B_CHUNK), 0)
    colm = lax.broadcasted_iota(jnp.int32, (B_CHUNK, B_CHUNK), 1)
    keep = (colm <= row) & (row < n_mix)
    rows = min(tb, B_CHUNK)
    for g in range(B_GROUPS):
        gs = slice(g * B_GROUP_DIM, (g + 1) * B_GROUP_DIM)
        w = jnp.where(keep, ws_ref[g], 0.0).astype(BF16)
        bias = bst_ref[:, g:g + 1]
        for c in range(max(tb // B_CHUNK, 1)):
            rs = slice(c * B_CHUNK, c * B_CHUNK + rows)
            vg = vn[rs, gs].astype(BF16)
            if rows < B_CHUNK:
                vg = jnp.concatenate([vg, jnp.zeros((B_CHUNK - rows, B_GROUP_DIM), BF16)], axis=0)
            mix = jnp.dot(w, vg, preferred_element_type=F32) + bias
            o_ref[0, rs, gs] = (gate[rs, gs] * mix[:rows]).astype(o_ref.dtype)


def _gmlp(p0, w_s, b_s, ln_v_g, ln_v_b, *, tb, want_vn):
    b, t, _ = p0.shape
    tb = min(tb, t)
    n_mix = min(t, B_CHUNK)
    base = 4 * A_WIDTH // B_WIDTH

    def col(j):
        return pl.BlockSpec((1, tb, B_WIDTH), lambda bb, i: (bb, i, base + j))

    out_specs = [pl.BlockSpec((1, tb, B_WIDTH), lambda bb, i: (bb, i, 0))]
    out_shape = [jax.ShapeDtypeStruct((b, t, B_WIDTH), BF16)]
    if want_vn:
        out_specs.append(pl.BlockSpec((1, tb, B_WIDTH), lambda bb, i: (bb, i, 0)))
        out_shape.append(jax.ShapeDtypeStruct((b, t, B_WIDTH), F32))
    res = pl.pallas_call(
        functools.partial(_gmlp_kernel, n_mix=n_mix, tb=tb),
        grid=(b, t // tb),
        in_specs=[col(0), col(1), col(2),
                  pl.BlockSpec((B_GROUPS, B_CHUNK, B_CHUNK), lambda bb, i: (0, 0, 0)),
                  pl.BlockSpec((B_CHUNK, B_GROUPS), lambda bb, i: (0, 0)),
                  pl.BlockSpec((1, B_WIDTH), lambda bb, i: (0, 0)),
                  pl.BlockSpec((1, B_WIDTH), lambda bb, i: (0, 0))],
        out_specs=out_specs,
        out_shape=out_shape,
        compiler_params=_cparams(("parallel", "parallel")),
        name="gmlp",
    )(p0, p0, p0, w_s, b_s.T, ln_v_g.reshape(1, -1), ln_v_b.reshape(1, -1))
    return res if want_vn else res[0]


def _outproj_kernel(*refs, widths):
    o_refs = refs[:len(widths)]
    w_ref, x_ref, mod_ref, g_ref, b_ref, out_ref = refs[len(widths):]
    y = None
    off = 0
    for o_ref, kw in zip(o_refs, widths):
        part = jnp.dot(o_ref[0], w_ref[off:off + kw, :], preferred_element_type=F32)
        y = part if y is None else y + part
        off += kw
    r = DN_ALPHA * x_ref[0] + (1.0 + mod_ref[0, 2:3, :]) * y
    out_ref[0] = _layer_norm(r, g_ref[...], b_ref[...])


def _outproj(os_, w, x, mod, ln_g, ln_b, *, tm):
    b, t, d = x.shape
    tm = min(tm, t)
    widths = tuple(o.shape[-1] for o in os_)
    ktot = sum(widths)
    in_specs = [pl.BlockSpec((1, tm, kw), lambda bb, i: (bb, i, 0)) for kw in widths]
    in_specs += [
        pl.BlockSpec((ktot, d), lambda bb, i: (0, 0), pipeline_mode=pl.Buffered(1)),
        pl.BlockSpec((1, tm, d), lambda bb, i: (bb, i, 0)),
        pl.BlockSpec((1, 3, d), lambda bb, i: (bb, 0, 0)),
        pl.BlockSpec((1, d), lambda bb, i: (0, 0)),
        pl.BlockSpec((1, d), lambda bb, i: (0, 0)),
    ]
    return pl.pallas_call(
        functools.partial(_outproj_kernel, widths=widths),
        grid=(b, t // tm),
        in_specs=in_specs,
        out_specs=pl.BlockSpec((1, tm, d), lambda bb, i: (bb, i, 0)),
        out_shape=jax.ShapeDtypeStruct((b, t, d), F32),
        compiler_params=_cparams(("parallel", "parallel")),
        name="outproj",
    )(*os_, w, x, mod, ln_g.reshape(1, d), ln_b.reshape(1, d))


def _delta_kernel(q_ref, k_ref, v_ref, z_ref, ba_ref, ad_ref, gn_ref, s0_ref, o_ref, s_ref, gt_scr,
                  *, heads, nblk, blk, t_valid):
    hg = pl.program_id(1)
    n = pl.program_id(2)
    tb = nblk * blk

    @pl.when(n == 0)
    def _():
        s_ref[...] = s0_ref[...]

    def head_cols(j):
        return slice(j * C_HEAD_DIM, (j + 1) * C_HEAD_DIM)

    ba = ba_ref[0]
    beta_all = jax.nn.sigmoid(ba)
    xa = ba + ad_ref[1:2, :]
    softplus = jnp.maximum(xa, 0.0) + jnp.log1p(jnp.exp(-jnp.abs(xa)))
    la_all = -jnp.exp(ad_ref[0:1, :]) * softplus
    if t_valid < tb:
        rowv = lax.broadcasted_iota(jnp.int32, (tb, LANES), 0) < t_valid
        beta_all = jnp.where(rowv, beta_all, 0.0)
        la_all = jnp.where(rowv, la_all, 0.0)

    lane = lax.broadcasted_iota(jnp.int32, (blk, LANES), 1)
    ri = lax.broadcasted_iota(jnp.int32, (blk, blk), 0)
    ci = lax.broadcasted_iota(jnp.int32, (blk, blk), 1)
    incl = ci <= ri
    strict = ci < ri
    ltri = incl.astype(F32)
    eye = (ci == ri).astype(F32)
    gn = gn_ref[...]

    items = [(nb, hh) for nb in range(nblk) for hh in range(heads)]
    rows = [slice(nb * blk, (nb + 1) * blk) for nb in range(nblk)]
    g_blk = []
    for nb in range(nblk):
        g_all = jnp.dot(ltri, la_all[rows[nb]], preferred_element_type=F32, precision=lax.Precision.HIGHEST)
        gt_scr[nb] = g_all.T
        g_blk.append(g_all)
    beta_c, g_c, g_last, dec, eg = {}, {}, {}, {}, {}
    for it in items:
        nb, hh = it
        hidx = hg * heads + hh
        beta_c[it] = jnp.sum(jnp.where(lane == hidx, beta_all[rows[nb]], 0.0), axis=1, keepdims=True)
        g_c[it] = jnp.sum(jnp.where(lane == C_V_HEADS + hidx, g_blk[nb], 0.0), axis=1, keepdims=True)
        g_r = gt_scr[nb, pl.ds(C_V_HEADS + hidx, 1), :]
        g_last[it] = g_c[it][blk - 1:blk, :]
        dec[it] = jnp.where(incl, jnp.exp(g_c[it] - g_r), 0.0)
        eg[it] = jnp.exp(g_c[it])
    kb, qk, pw, tinv = {}, {}, {}, {}
    for nb in range(nblk):
        for j in range(heads // 2):
            h0, h1 = (nb, 2 * j), (nb, 2 * j + 1)
            kbf = k_ref[0, rows[nb], head_cols(j)]
            k = kbf.astype(F32)
            kb[h0] = k * beta_c[h0]
            kb[h1] = k * beta_c[h1]
            lhs = jnp.concatenate([q_ref[0, rows[nb], head_cols(j)], kb[h0].astype(BF16), kb[h1].astype(BF16)],
                                  axis=0)
            qa = lax.dot_general(lhs, kbf, (((1,), (1,)), ((), ())), preferred_element_type=F32)
            for m, it in enumerate((h0, h1)):
                qk[it] = (qa[:blk] * dec[it]).astype(BF16)
                pw[it] = jnp.where(strict, qa[(m + 1) * blk:(m + 2) * blk] * dec[it], 0.0)
                tinv[it] = eye - pw[it]
    span = 2
    while span < blk:
        for it in items:
            pwb = pw[it].astype(BF16)
            pw[it] = jnp.dot(pwb, pwb, preferred_element_type=F32)
        for it in items:
            tinv[it] = tinv[it] + jnp.dot(tinv[it].astype(BF16), pw[it].astype(BF16), preferred_element_type=F32)
        span *= 2
    wq, u = {}, {}
    for it in items:
        nb, hh = it
        v = v_ref[0, rows[nb], head_cols(hh)].astype(F32)
        rhs = jnp.concatenate([kb[it] * eg[it], v * beta_c[it]], axis=1).astype(BF16)
        wu = jnp.dot(tinv[it].astype(BF16), rhs, preferred_element_type=F32)
        qd = q_ref[0, rows[nb], head_cols(hh // 2)].astype(F32) * eg[it]
        wq[it] = jnp.concatenate([wu[:, :C_HEAD_DIM], qd], axis=0).astype(BF16)
        u[it] = wu[:, C_HEAD_DIM:]
    s = [s_ref[0, hh] for hh in range(heads)]
    for nb in range(nblk):
        ws_qs = [jnp.dot(wq[(nb, hh)], s[hh].astype(BF16), preferred_element_type=F32) for hh in range(heads)]
        vnb = [(u[(nb, hh)] - ws_qs[hh][:blk]).astype(BF16) for hh in range(heads)]
        for hh in range(heads):
            it = (nb, hh)
            k = k_ref[0, rows[nb], head_cols(hh // 2)].astype(F32)
            k_dec = (k * jnp.exp(g_last[it] - g_c[it])).astype(BF16)
            s[hh] = s[hh] * jnp.exp(g_last[it]) + lax.dot_general(
                k_dec, vnb[hh], (((0,), (0,)), ((), ())), preferred_element_type=F32)
        for hh in range(heads):
            o = ws_qs[hh][blk:] + jnp.dot(qk[(nb, hh)], vnb[hh], preferred_element_type=F32)
            on = o * lax.rsqrt(jnp.mean(o * o, axis=-1, keepdims=True) + NORM_EPS) * gn
            hs = slice(hh * C_HEAD_DIM, (hh + 1) * C_HEAD_DIM)
            o_ref[0, rows[nb], hs] = (on * _silu(z_ref[0, rows[nb], hs].astype(F32))).astype(o_ref.dtype)
    for hh in range(heads):
        s_ref[0, hh] = s[hh]


def _delta(p1, ba, s0, ad_rows, o_norm_g, *, heads, nblk, t_valid):
    b, t, _ = p1.shape
    blk = CHUNK
    tb = nblk * blk
    cq = heads // 2 * C_HEAD_DIM
    cv = heads * C_HEAD_DIM
    ko, vo, zo = C_QK_WIDTH // cq, 2 * C_QK_WIDTH // cv, C_CONV_CH // cv

    def tok(width, off):
        return pl.BlockSpec((1, tb, width), lambda bb, hg, n: (bb, n, off + hg))

    state_spec = pl.BlockSpec((1, heads, C_HEAD_DIM, C_HEAD_DIM), lambda bb, hg, n: (bb, hg, 0, 0))
    return pl.pallas_call(
        functools.partial(_delta_kernel, heads=heads, nblk=nblk, blk=blk, t_valid=t_valid),
        grid=(b, C_V_HEADS // heads, t // tb),
        in_specs=[tok(cq, 0), tok(cq, ko), tok(cv, vo), tok(cv, zo),
                  pl.BlockSpec((1, tb, LANES), lambda bb, hg, n: (bb, n, 0)),
                  pl.BlockSpec((2, LANES), lambda bb, hg, n: (0, 0)),
                  pl.BlockSpec((1, C_HEAD_DIM), lambda bb, hg, n: (0, 0)),
                  state_spec],
        out_specs=[pl.BlockSpec((1, tb, cv), lambda bb, hg, n: (bb, n, hg)), state_spec],
        out_shape=[jax.ShapeDtypeStruct((b, t, C_V_WIDTH), BF16),
                   jax.ShapeDtypeStruct((b, C_V_HEADS, C_HEAD_DIM, C_HEAD_DIM), F32)],
        scratch_shapes=[pltpu.VMEM((nblk, LANES, blk), F32)],
        compiler_params=_cparams(("parallel", "parallel", "arbitrary")),
        name="delta",
    )(p1, p1, p1, p1, ba, ad_rows, o_norm_g.reshape(1, C_HEAD_DIM), s0)


def _trunk(x, mods, wts, cache_k, cache_v, conv_left, s0, *, prompt):
    (w_in_ab, bias_blocks, rel_table, ln_v_g, ln_v_b, w_s, b_s, w_out_ab, w_in_dn, w_extra, w_conv, ad_rows,
     o_norm_g, w_out_dn, ln_g, ln_b) = wts
    b, t, _ = x.shape
    p0 = _inproj(x, mods[0], w_in_ab, tm=1024, tn=1024, cm=256)
    if prompt:
        o_a = _attn_prompt(p0, bias_blocks)
        o_b = _gmlp(p0, w_s, b_s, ln_v_g, ln_v_b, tb=256, want_vn=False)
        v_n = None
        keep = min(A_BAND * CHUNK, t)
    else:
        o_a = _attn_sample(p0, cache_k, cache_v, rel_table)
        o_b, v_n = _gmlp(p0, w_s, b_s, ln_v_g, ln_v_b, tb=256, want_vn=True)
        keep = t
    new_k = p0[:, t - keep:, A_WIDTH:2 * A_WIDTH].astype(F32).reshape(b, keep, A_HEADS, A_HEAD_DIM)
    new_v = p0[:, t - keep:, 2 * A_WIDTH:3 * A_WIDTH].astype(F32).reshape(b, keep, A_HEADS, A_HEAD_DIM)
    x1 = _outproj([o_a, o_b], w_out_ab, x, mods[0], ln_g[0], ln_b[0], tm=512)
    if conv_left is None:
        left8 = jnp.zeros((b, HALO_ROWS, C_CONV_CH), F32)
    else:
        left8 = jnp.pad(conv_left, ((0, 0), (HALO_ROWS - (C_CONV - 1), 0), (0, 0)))
    p1, ba, tail = _inproj(x1, mods[1], w_in_dn, (w_extra, w_conv, left8), tm=1024, tn=1024, cm=256, cn=1024)
    new_conv = tail[:, -1, HALO_ROWS - (C_CONV - 1):, :]
    if s0 is None:
        s0 = jnp.zeros((b, C_V_HEADS, C_HEAD_DIM, C_HEAD_DIM), F32)
    t_pad = -t % CHUNK
    if t_pad:
        p1 = jnp.pad(p1, ((0, 0), (0, t_pad), (0, 0)))
        ba = jnp.pad(ba, ((0, 0), (0, t_pad), (0, 0)))
    o_c, s_new = _delta(p1, ba, s0, ad_rows, o_norm_g,
                        heads=8, nblk=2 if (t + t_pad) % (2 * CHUNK) == 0 else 1, t_valid=t)
    if t_pad:
        o_c = o_c[:, :t]
    x2 = _outproj([o_c], w_out_dn, x1, mods[1], ln_g[1], ln_b[1], tm=512)
    return x2, new_k, new_v, v_n, new_conv, s_new


def kernel(x_prompt, x_sample, cache_a_k, cache_a_v, state_c_conv, state_c_s, c_prompt, c_sample, w_ada, b_ada,
           ln_g, ln_b, w_in_ab, rel_table, ln_v_g, ln_v_b, w_s, b_s, w_out_ab, w_in_dn, w_conv, a_log, dt_bias,
           o_norm_g, w_out_dn):
    bp = c_prompt.shape[0]
    bs = c_sample.shape[0]
    c_all = jnp.concatenate([c_prompt, c_sample], axis=0)
    c_all = jnp.pad(c_all, ((0, -(bp + bs) % 16), (0, 0)))
    ada = _ada(c_all, w_ada, b_ada)
    mods_p = [ada[l, :bp].reshape(bp, 3, D_MODEL) for l in range(DEPTH)]
    mods_s = [ada[l, bp:bp + bs].reshape(bs, 3, D_MODEL) for l in range(DEPTH)]

    extra = w_in_dn[:, C_MAIN:]
    w_extra = jnp.pad(extra, ((0, 0), (0, LANES - extra.shape[1]))).astype(BF16)
    ad_rows = jnp.zeros((2, LANES), F32)
    ad_rows = ad_rows.at[0, C_V_HEADS:2 * C_V_HEADS].set(a_log).at[1, C_V_HEADS:2 * C_V_HEADS].set(dt_bias)
    wts = (w_in_ab.astype(BF16), _band_bias_tile(rel_table), rel_table, ln_v_g, ln_v_b, w_s, b_s,
           w_out_ab.astype(BF16), w_in_dn[:, :C_MAIN].astype(BF16), w_extra, w_conv, ad_rows, o_norm_g,
           w_out_dn.astype(BF16), ln_g, ln_b)

    y_p, p_a_k, p_a_v, _, p_c_conv, p_c_s = _trunk(x_prompt, mods_p, wts, None, None, None, None, prompt=True)
    y_s, s_a_k, s_a_v, s_b_v, s_c_conv, s_c_s = _trunk(x_sample, mods_s, wts, cache_a_k, cache_a_v, state_c_conv,
                                                       state_c_s, prompt=False)
    s_b_v = s_b_v.reshape(bs, -1, B_GROUPS, B_GROUP_DIM)
    return (y_p, y_s, p_a_k, p_a_v, p_c_conv, p_c_s, s_a_k, s_a_v, s_b_v, s_c_conv, s_c_s)
```

```python
import functools

import jax
import jax.numpy as jnp
from jax import lax
from jax.experimental import pallas as pl
from jax.experimental.pallas import tpu as pltpu

F32 = jnp.float32
BF16 = jnp.bfloat16

D_MODEL = 2048
DEPTH = 2
CHUNK = 64
A_HEADS = 16
A_HEAD_DIM = 64
A_WIDTH = A_HEADS * A_HEAD_DIM
A_BAND = 8
REL_MAX = 128
B_GROUPS = 8
B_GROUP_DIM = 128
B_WIDTH = B_GROUPS * B_GROUP_DIM
B_CHUNK = 128
C_QK_HEADS = 16
C_V_HEADS = 32
C_HEAD_DIM = 128
C_QK_WIDTH = C_QK_HEADS * C_HEAD_DIM
C_V_WIDTH = C_V_HEADS * C_HEAD_DIM
C_CONV = 4
C_CONV_CH = 2 * C_QK_WIDTH + C_V_WIDTH
C_MAIN = C_CONV_CH + C_V_WIDTH
DN_ALPHA = (2 * DEPTH) ** 0.25
LN_EPS = 1e-5
NORM_EPS = 1e-6
NEG_BIG = -1e30

LANES = 128
HALO_ROWS = 8
VMEM_LIMIT = 56 * 1024 * 1024


def _cparams(sem):
    return pltpu.CompilerParams(dimension_semantics=sem, vmem_limit_bytes=VMEM_LIMIT)


def _silu(x):
    return x * jax.nn.sigmoid(x)


def _gelu(x):
    return 0.5 * x * (1.0 + lax.erf(x * (2.0 ** -0.5)))


def _layer_norm(x, g, b):
    mu = jnp.mean(x, axis=-1, keepdims=True)
    xc = x - mu
    var = jnp.mean(xc * xc, axis=-1, keepdims=True)
    return xc * lax.rsqrt(var + LN_EPS) * g + b


def _ada_kernel(c_ref, w_ref, b_ref, o_ref):
    c = c_ref[...].astype(BF16)
    w = w_ref[0].astype(BF16)
    o_ref[0] = jnp.dot(c, w, preferred_element_type=F32) + b_ref[0]


def _ada(c_all, w_ada, b_ada):
    rows, d = c_all.shape
    depth, _, n = w_ada.shape
    tn = 768
    return pl.pallas_call(
        _ada_kernel,
        grid=(depth, n // tn),
        in_specs=[
            pl.BlockSpec((rows, d), lambda l, j: (0, 0)),
            pl.BlockSpec((1, d, tn), lambda l, j: (l, 0, j)),
            pl.BlockSpec((1, 1, tn), lambda l, j: (l, 0, j)),
        ],
        out_specs=pl.BlockSpec((1, rows, tn), lambda l, j: (l, 0, j)),
        out_shape=jax.ShapeDtypeStruct((depth, rows, n), F32),
        compiler_params=_cparams(("parallel", "parallel")),
        name="ada",
    )(c_all, w_ada, b_ada.reshape(depth, 1, n))


def _inproj_kernel(*refs, bt, tm, tn, cm, cn, dn):
    if dn:
        x_ref, mod_ref, w_ref, we_ref, ad_ref, wc_ref, cl_ref, o_ref, oe_ref, tail_ref, h_scr, halo_scr = refs
    else:
        x_ref, mod_ref, w_ref, o_ref, h_scr = refs
    i = pl.program_id(1)
    j = pl.program_id(2)
    rows = bt * tm

    @pl.when(j == 0)
    def _():
        h = x_ref[...] * (1.0 + mod_ref[:, 1:2, :]) + mod_ref[:, 0:1, :]
        hb = h.reshape(rows, h.shape[-1]).astype(BF16)
        h_scr[...] = hb
        if dn:
            raw = jnp.dot(hb, we_ref[...], preferred_element_type=F32)
            beta = jax.nn.sigmoid(raw)
            xa = raw + ad_ref[1:2, :]
            softplus = jnp.maximum(xa, 0.0) + jnp.log1p(jnp.exp(-jnp.abs(xa)))
            la = -jnp.exp(ad_ref[0:1, :]) * softplus
            gb = min(CHUNK, tm)
            grp = min(rows, 4 * CHUNK)
            r = lax.broadcasted_iota(jnp.int32, (grp, grp), 0)
            c = lax.broadcasted_iota(jnp.int32, (grp, grp), 1)
            sh = gb.bit_length() - 1
            same = lax.shift_right_logical(r, sh) == lax.shift_right_logical(c, sh)
            csum = ((c <= r) & same).astype(F32)
            g = jnp.concatenate(
                [jnp.dot(csum, la[g0:g0 + grp], preferred_element_type=F32, precision=lax.Precision.HIGHEST)
                 for g0 in range(0, rows, grp)], axis=0)
            lane = lax.broadcasted_iota(jnp.int32, raw.shape, 1)
            bg = jnp.where(lane < C_V_HEADS, beta, g)
            oe_ref[...] = bg.reshape(bt, tm, bg.shape[-1])

    def plain():
        acc = jnp.dot(h_scr[...], w_ref[...], preferred_element_type=F32)
        o_ref[...] = acc.reshape(bt, tm, tn).astype(o_ref.dtype)

    if not dn:
        plain()
        return

    n_conv = C_CONV_CH // tn

    def conv_tiles():
        is_qk = j < (2 * C_QK_WIDTH) // tn
        qscale = jnp.where(j < C_QK_WIDTH // tn, C_HEAD_DIM ** -0.5, 1.0)
        row8 = lax.broadcasted_iota(jnp.int32, (HALO_ROWS, cn), 0)
        ncp = tn // cn

        def cols(nt):
            return slice(nt * cn, (nt + 1) * cn)

        def epilogue(seg, prev, nt):
            taps = wc_ref[:, cols(nt)]
            y = taps[C_CONV - 1:C_CONV] * seg
            for delay in range(1, C_CONV):
                rolled = pltpu.roll(seg, delay, 0)
                first = jnp.where(row8 < delay, pltpu.roll(prev, delay, 0), rolled[:HALO_ROWS])
                shifted = jnp.concatenate([first, rolled[HALO_ROWS:]], axis=0)
                y = y + taps[C_CONV - 1 - delay:C_CONV - delay] * shifted
            s = _silu(y)
            outs = []
            for hd in range(cn // C_HEAD_DIM):
                sg = s[:, hd * C_HEAD_DIM:(hd + 1) * C_HEAD_DIM]
                f = lax.rsqrt(jnp.sum(sg * sg, axis=-1, keepdims=True) + NORM_EPS) * qscale
                outs.append(sg * jnp.where(is_qk, f, 1.0))
            return jnp.concatenate(outs, axis=1).astype(o_ref.dtype)

        if bt == 1:
            @pl.when(i == 0)
            def _():
                halo_scr[j] = cl_ref[0]

            prev = {nt: halo_scr[j, :, cols(nt)] for nt in range(ncp)}
            pieces = [(c, nt) for c in range(tm // cm) for nt in range(ncp)]
            pending = None
            for p in range(len(pieces) + 1):
                issued = None
                if p < len(pieces):
                    c, nt = pieces[p]
                    issued = (c, nt, jnp.dot(h_scr[c * cm:(c + 1) * cm, :], w_ref[:, cols(nt)],
                                             preferred_element_type=F32))
                if pending is not None:
                    c, nt, a = pending
                    o_ref[0, c * cm:(c + 1) * cm, cols(nt)] = epilogue(a, prev[nt], nt)
                    prev[nt] = a[cm - HALO_ROWS:cm]
                pending = issued
            for nt in range(ncp):
                halo_scr[j, :, cols(nt)] = prev[nt]
                tail_ref[0, 0, :, cols(nt)] = prev[nt]
        else:
            for nt in range(ncp):
                acc = jnp.dot(h_scr[...], w_ref[:, cols(nt)], preferred_element_type=F32)
                for bi in range(bt):
                    seg = acc[bi * tm:(bi + 1) * tm]
                    o_ref[bi, :, cols(nt)] = epilogue(seg, cl_ref[bi, :, cols(nt)], nt)
                    tail_ref[bi, 0, :, cols(nt)] = seg[tm - HALO_ROWS:tm]

    pl.when(j < n_conv)(conv_tiles)
    pl.when(j >= n_conv)(plain)


def _inproj(x, mod, w, dn_args=None, *, tm, tn, cm, cn=256):
    b, t, d = x.shape
    n = w.shape[1]
    tm = min(tm, t)
    bt = b if t == tm and b * t <= 1024 else 1
    cm = min(cm, tm)
    dn = dn_args is not None
    in_specs = [
        pl.BlockSpec((bt, tm, d), lambda bb, i, j: (bb, i, 0)),
        pl.BlockSpec((bt, 3, d), lambda bb, i, j: (bb, 0, 0)),
        pl.BlockSpec((d, tn), lambda bb, i, j: (0, j)),
    ]
    out_specs = [pl.BlockSpec((bt, tm, tn), lambda bb, i, j: (bb, i, j))]
    out_shape = [jax.ShapeDtypeStruct((b, t, n), BF16)]
    scratch = [pltpu.VMEM((bt * tm, d), BF16)]
    args = [x, mod, w]
    if dn:
        w_extra, ad_rows, w_conv, conv_left8 = dn_args
        ne = w_extra.shape[1]
        last = C_CONV_CH // tn - 1

        def cj(j):
            return jnp.minimum(j, last)

        in_specs += [
            pl.BlockSpec((d, ne), lambda bb, i, j: (0, 0)),
            pl.BlockSpec((2, ne), lambda bb, i, j: (0, 0)),
            pl.BlockSpec((C_CONV, tn), lambda bb, i, j: (0, cj(j))),
            pl.BlockSpec((bt, HALO_ROWS, tn), lambda bb, i, j: (bb, 0, cj(j))),
        ]
        out_specs += [
            pl.BlockSpec((bt, tm, ne), lambda bb, i, j: (bb, i, 0)),
            pl.BlockSpec((bt, 1, HALO_ROWS, tn), lambda bb, i, j: (bb, i, 0, cj(j))),
        ]
        out_shape += [jax.ShapeDtypeStruct((b, t, ne), F32),
                      jax.ShapeDtypeStruct((b, t // tm, HALO_ROWS, C_CONV_CH), F32)]
        scratch.append(pltpu.VMEM((C_CONV_CH // tn, HALO_ROWS, tn), F32))
        args += [w_extra, ad_rows, w_conv, conv_left8]
    res = pl.pallas_call(
        functools.partial(_inproj_kernel, bt=bt, tm=tm, tn=tn, cm=cm, cn=min(cn, tn), dn=dn),
        grid=(b // bt, t // tm, n // tn),
        in_specs=in_specs,
        out_specs=out_specs,
        out_shape=out_shape,
        scratch_shapes=scratch,
        compiler_params=_cparams(("parallel", "arbitrary", "arbitrary")),
        name="inproj_dn" if dn else "inproj",
    )(*args)
    return res if dn else res[0]


ATTN_TQ = 256
ATTN_SUB = 128
ATTN_WIN = ATTN_SUB + A_BAND * CHUNK
ATTN_HEADS = 4


def _attn_kernel(q_ref, k0_ref, k1_ref, k2_ref, v0_ref, v1_ref, v2_ref, z_ref, bias_ref, o_ref, bias_scr):
    i = pl.program_id(2)
    nsub = ATTN_TQ // ATTN_SUB

    @pl.when(i <= 2)
    def _():
        for sb in range(nsub):
            col = lax.broadcasted_iota(jnp.int32, (1, ATTN_WIN), 1) + sb * ATTN_SUB
            missing = col < (2 - i) * ATTN_TQ
            for h in range(ATTN_HEADS):
                bias_scr[sb * ATTN_HEADS + h] = jnp.where(missing, NEG_BIG, bias_ref[h])

    q = q_ref[0] * (A_HEAD_DIM ** -0.5)
    k = jnp.concatenate([k0_ref[0], k1_ref[0], k2_ref[0]], axis=0)
    v = jnp.concatenate([v0_ref[0], v1_ref[0], v2_ref[0]], axis=0)
    items = [(h, sb) for h in range(ATTN_HEADS) for sb in range(nsub)]
    low = lax.broadcasted_iota(jnp.int32, (1, LANES), 1) < A_HEAD_DIM

    def pair(h):
        return slice(h // 2 * LANES, (h // 2 + 1) * LANES)

    def own(h):
        return low if h % 2 == 0 else jnp.logical_not(low)

    q_own = [jnp.where(own(h), q[:, pair(h)], jnp.zeros((), BF16)) for h in range(ATTN_HEADS)]
    v_ext = [jnp.where(own(h), v[:, pair(h)], jnp.ones((), BF16)) for h in range(ATTN_HEADS)]

    def scores(it):
        h, sb = it
        qs = q_own[h][sb * ATTN_SUB:(sb + 1) * ATTN_SUB]
        ks = k[sb * ATTN_SUB:sb * ATTN_SUB + ATTN_WIN, pair(h)]
        s = lax.dot_general(qs, ks, (((1,), (1,)), ((), ())), preferred_element_type=F32)
        return s + bias_scr[sb * ATTN_HEADS + h]

    def weights(s):
        return jnp.exp(s - jnp.max(s, axis=-1, keepdims=True)).astype(BF16)

    def attend(it, p):
        h, sb = it
        return jnp.dot(p, v_ext[h][sb * ATTN_SUB:sb * ATTN_SUB + ATTN_WIN], preferred_element_type=F32)

    ss = [scores(it) for it in items]
    ps = [weights(s) for s in ss]
    ol = {it: attend(it, p) for it, p in zip(items, ps)}

    def normalised(h0, sb):
        e0, e1 = ol[(h0, sb)], ol[(h0 + 1, sb)]
        return jnp.where(low, e0 / pltpu.roll(e0, A_HEAD_DIM, 1), e1 / pltpu.roll(e1, A_HEAD_DIM, 1))

    o = jnp.concatenate(
        [jnp.concatenate([normalised(h0, sb) for sb in range(nsub)], axis=0) for h0 in range(0, ATTN_HEADS, 2)],
        axis=1)
    o_ref[0] = (o * _silu(z_ref[0].astype(F32))).astype(o_ref.dtype)


def _band_bias_tile(rel_table):
    h = rel_table.shape[0]
    far = A_BAND * CHUNK
    n_diag = ATTN_WIN + ATTN_SUB - 1
    n_const = far + ATTN_SUB - 1 - REL_MAX + 1
    w = jnp.concatenate([jnp.broadcast_to(rel_table[:, 2 * REL_MAX:], (h, n_const)),
                         rel_table[:, 2 * REL_MAX - 1::-1][:, :n_diag - n_const]], axis=1).astype(F32)
    tiled =jnp.tile(jnp.pad(w, ((0, 0), (0, 1))), (1, ATTN_SUB))[:, :ATTN_SUB * n_diag]
    toep = tiled.reshape(h, ATTN_SUB, n_diag)[:, :, ATTN_SUB - 1:ATTN_SUB - 1 + ATTN_WIN]
    row = jnp.arange(ATTN_SUB)[:, None]
    colx = jnp.arange(ATTN_WIN)[None, :]
    rel = colx - (row // CHUNK) * CHUNK
    band = (rel >= 0) & (rel < (A_BAND + 1) * CHUNK)
    return jnp.where(band[None], toep, NEG_BIG)


def _attn_prompt(p0, bias):
    b, t, _ = p0.shape
    tq = ATTN_TQ
    width = ATTN_HEADS * A_HEAD_DIM
    per = A_WIDTH // width
    qo, ko, vo, zo = 0, per, 2 * per, 3 * per

    def blk(off, back):
        return pl.BlockSpec((1, tq, width), lambda hg, bb, i: (bb, jnp.maximum(i - back, 0), off + hg))

    return pl.pallas_call(
        _attn_kernel,
        grid=(per, b, t // tq),
        in_specs=[blk(qo, 0), blk(ko, 2), blk(ko, 1), blk(ko, 0), blk(vo, 2), blk(vo, 1), blk(vo, 0), blk(zo, 0),
                  pl.BlockSpec((ATTN_HEADS, ATTN_SUB, ATTN_WIN), lambda hg, bb, i: (hg, 0, 0))],
        out_specs=pl.BlockSpec((1, tq, width), lambda hg, bb, i: (bb, i, hg)),
        out_shape=jax.ShapeDtypeStruct((b, t, A_WIDTH), BF16),
        scratch_shapes=[pltpu.VMEM((ATTN_TQ // ATTN_SUB * ATTN_HEADS, ATTN_SUB, ATTN_WIN), F32)],
        compiler_params=_cparams(("parallel", "parallel", "arbitrary")),
        name="attn_prompt",
    )(p0, p0, p0, p0, p0, p0, p0, p0, bias)


def _attn_sample_kernel(q_ref, k_ref, v_ref, z_ref, ck_ref, cv_ref, bias_ref, o_ref, *, pad_rows):
    q = q_ref[0]
    zpad = jnp.zeros((pad_rows, A_WIDTH), BF16)
    k = jnp.concatenate([ck_ref[0], k_ref[0], zpad], axis=0)
    v = jnp.concatenate([cv_ref[0], v_ref[0], zpad], axis=0)
    outs = []
    for h in range(A_HEADS):
        sl = slice(h * A_HEAD_DIM, (h + 1) * A_HEAD_DIM)
        s = lax.dot_general(q[:, sl], k[:, sl], (((1,), (1,)), ((), ())), preferred_element_type=F32)
        s = s * (A_HEAD_DIM ** -0.5) + bias_ref[h]
        m = jnp.max(s, axis=-1, keepdims=True)
        p = jnp.exp(s - m)
        l = jnp.sum(p, axis=-1, keepdims=True)
        o = jnp.dot(p.astype(BF16), v[:, sl], preferred_element_type=F32)
        outs.append(o / l)
    o = jnp.concatenate(outs, axis=1)
    o_ref[0] = (o * _silu(z_ref[0].astype(F32))).astype(o_ref.dtype)


def _attn_sample(p0, cache_k, cache_v, rel_table):
    b, t, _ = p0.shape
    n_cache = cache_k.shape[1]
    n_keys = n_cache + t
    n_pad = -n_keys % LANES
    dist = jnp.arange(t)[:, None] + n_cache - jnp.arange(n_keys)[None, :]
    bias = rel_table[:, jnp.clip(dist, -REL_MAX, REL_MAX) + REL_MAX].astype(F32)
    bias = jnp.pad(bias, ((0, 0), (0, 0), (0, n_pad)), constant_values=NEG_BIG)
    ck = cache_k.reshape(b, n_cache, A_WIDTH).astype(BF16)
    cv = cache_v.reshape(b, n_cache, A_WIDTH).astype(BF16)

    def col(j):
        return pl.BlockSpec((1, t, A_WIDTH), lambda bb: (bb, 0, j))

    cache_spec = pl.BlockSpec((1, n_cache, A_WIDTH), lambda bb: (bb, 0, 0))
    return pl.pallas_call(
        functools.partial(_attn_sample_kernel, pad_rows=n_pad),
        grid=(b,),
        in_specs=[col(0), col(1), col(2), col(3), cache_spec, cache_spec,
                  pl.BlockSpec((A_HEADS, t, n_keys + n_pad), lambda bb: (0, 0, 0))],
        out_specs=pl.BlockSpec((1, t, A_WIDTH), lambda bb: (bb, 0, 0)),
        out_shape=jax.ShapeDtypeStruct((b, t, A_WIDTH), BF16),
        compiler_params=_cparams(("parallel",)),
        name="attn_sample",
    )(p0, p0, p0, p0, ck, cv, bias)


def _gmlp_kernel(u_ref, v_ref, z_ref, ws_ref, bst_ref, g_ref, b_ref, o_ref, *vn_out, n_mix, tb):
    vn = _layer_norm(_gelu(v_ref[0].astype(F32)), g_ref[...], b_ref[...])
    if vn_out:
        vn_out[0][0] = vn
    z = z_ref[0].astype(F32)
    gate = _gelu(u_ref[0].astype(F32)) * _silu(z)
    row = lax.broadcasted_iota(jnp.int32, (B_CHUNK, B_CHUNK), 0)
    colm = lax.broadcasted_iota(jnp.int32, (B_CHUNK, B_CHUNK), 1)
    keep = (colm <= row) & (row < n_mix)
    rows = min(tb, B_CHUNK)
    for g in range(B_GROUPS):
        gs = slice(g * B_GROUP_DIM, (g + 1) * B_GROUP_DIM)
        w = jnp.where(keep, ws_ref[g], 0.0).astype(BF16)
        bias = bst_ref[:, g:g + 1]
        for c in range(max(tb // B_CHUNK, 1)):
            rs = slice(c * B_CHUNK, c * B_CHUNK + rows)
            vg = vn[rs, gs].astype(BF16)
            if rows < B_CHUNK:
                vg = jnp.concatenate([vg, jnp.zeros((B_CHUNK - rows, B_GROUP_DIM), BF16)], axis=0)
            mix = jnp.dot(w, vg, preferred_element_type=F32) + bias
            o_ref[0, rs, gs] = (gate[rs, gs] * mix[:rows]).astype(o_ref.dtype)


def _gmlp(p0, w_s, b_s, ln_v_g, ln_v_b, *, tb, want_vn):
    b, t, _ = p0.shape
    tb = min(tb, t)
    n_mix = min(t, B_CHUNK)
    base = 4 * A_WIDTH // B_WIDTH

    def col(j):
        return pl.BlockSpec((1, tb, B_WIDTH), lambda bb, i: (bb, i, base + j))

    out_specs = [pl.BlockSpec((1, tb, B_WIDTH), lambda bb, i: (bb, i, 0))]
    out_shape = [jax.ShapeDtypeStruct((b, t, B_WIDTH), BF16)]
    if want_vn:
        out_specs.append(pl.BlockSpec((1, tb, B_WIDTH), lambda bb, i: (bb, i, 0)))
        out_shape.append(jax.ShapeDtypeStruct((b, t, B_WIDTH), F32))
    res = pl.pallas_call(
        functools.partial(_gmlp_kernel, n_mix=n_mix, tb=tb),
        grid=(b, t // tb),
        in_specs=[col(0), col(1), col(2),
                  pl.BlockSpec((B_GROUPS, B_CHUNK, B_CHUNK), lambda bb, i: (0, 0, 0)),
                  pl.BlockSpec((B_CHUNK, B_GROUPS), lambda bb, i: (0, 0)),
                  pl.BlockSpec((1, B_WIDTH), lambda bb, i: (0, 0)),
                  pl.BlockSpec((1, B_WIDTH), lambda bb, i: (0, 0))],
        out_specs=out_specs,
        out_shape=out_shape,
        compiler_params=_cparams(("parallel", "parallel")),
        name="gmlp",
    )(p0, p0, p0, w_s, b_s.T, ln_v_g.reshape(1, -1), ln_v_b.reshape(1, -1))
    return res if want_vn else res[0]


def _outproj_kernel(*refs, widths):
    o_refs = refs[:len(widths)]
    w_ref, x_ref, mod_ref, g_ref, b_ref, out_ref = refs[len(widths):]
    y = None
    off = 0
    for o_ref, kw in zip(o_refs, widths):
        part = jnp.dot(o_ref[0], w_ref[off:off + kw, :], preferred_element_type=F32)
        y = part if y is None else y + part
        off += kw
    r = DN_ALPHA * x_ref[0] + (1.0 + mod_ref[0, 2:3, :]) * y
    out_ref[0] = _layer_norm(r, g_ref[...], b_ref[...])


def _outproj(os_, w, x, mod, ln_g, ln_b, *, tm):
    b, t, d = x.shape
    tm = min(tm, t)
    widths = tuple(o.shape[-1] for o in os_)
    ktot = sum(widths)
    in_specs = [pl.BlockSpec((1, tm, kw), lambda bb, i: (bb, i, 0)) for kw in widths]
    in_specs += [
        pl.BlockSpec((ktot, d), lambda bb, i: (0, 0), pipeline_mode=pl.Buffered(1)),
        pl.BlockSpec((1, tm, d), lambda bb, i: (bb, i, 0)),
        pl.BlockSpec((1, 3, d), lambda bb, i: (bb, 0, 0)),
        pl.BlockSpec((1, d), lambda bb, i: (0, 0)),
        pl.BlockSpec((1, d), lambda bb, i: (0, 0)),
    ]
    return pl.pallas_call(
        functools.partial(_outproj_kernel, widths=widths),
        grid=(b, t // tm),
        in_specs=in_specs,
        out_specs=pl.BlockSpec((1, tm, d), lambda bb, i: (bb, i, 0)),
        out_shape=jax.ShapeDtypeStruct((b, t, d), F32),
        compiler_params=_cparams(("parallel", "parallel")),
        name="outproj",
    )(*os_, w, x, mod, ln_g.reshape(1, d), ln_b.reshape(1, d))


def _delta_kernel(q_ref, k_ref, v_ref, z_ref, bg_ref, gn_ref, s0_ref, o_ref, s_ref, gt_scr,
                  *, heads, nblk, blk, group):
    hg = pl.program_id(1)
    n = pl.program_id(2)

    @pl.when(n == 0)
    def _():
        s_ref[...] = s0_ref[...]

    def head_cols(j):
        return slice(j * C_HEAD_DIM, (j + 1) * C_HEAD_DIM)

    lane = lax.broadcasted_iota(jnp.int32, (blk, LANES), 1)
    ri = lax.broadcasted_iota(jnp.int32, (blk, blk), 0)
    ci = lax.broadcasted_iota(jnp.int32, (blk, blk), 1)
    incl = ci <= ri
    strict = ci < ri
    eye = (ci == ri).astype(F32)
    gn = gn_ref[...]

    rows = [slice(nb * blk, (nb + 1) * blk) for nb in range(nblk)]
    hh_all = range(heads)
    beta_c, g_c, g_last, eg, kb, qk, amat, wq, u = {}, {}, {}, {}, {}, {}, {}, {}, {}

    def mm(a, b):
        return jnp.dot(a.astype(BF16), b.astype(BF16), preferred_element_type=F32)

    def times(a, b):
        return a + mm(a, b - eye)

    def prepare(nbs):
        dec = {}
        for nb in nbs:
            bg = bg_ref[0, rows[nb], :]
            gt_scr[nb] = bg.T
            for hh in hh_all:
                it = (nb, hh)
                hidx = hg * heads + hh
                beta_c[it] = jnp.sum(jnp.where(lane == hidx, bg, 0.0), axis=1, keepdims=True)
                g_c[it] = jnp.sum(jnp.where(lane == C_V_HEADS + hidx, bg, 0.0), axis=1, keepdims=True)
                g_last[it] = g_c[it][blk - 1:blk, :]
                eg[it] = jnp.exp(g_c[it])
        yield
        for nb in nbs:
            for hh in hh_all:
                g_r = gt_scr[nb, pl.ds(C_V_HEADS + hg * heads + hh, 1), :]
                dec[(nb, hh)] = jnp.where(incl, jnp.exp(g_c[(nb, hh)] - g_r), 0.0)
        yield
        for nb in nbs:
            for j in range(heads // 2):
                h0, h1 = (nb, 2 * j), (nb, 2 * j + 1)
                kbf = k_ref[0, rows[nb], head_cols(j)]
                k = kbf.astype(F32)
                kb[h0] = k * beta_c[h0]
                kb[h1] = k * beta_c[h1]
                lhs = jnp.concatenate(
                    [q_ref[0, rows[nb], head_cols(j)], kb[h0].astype(BF16), kb[h1].astype(BF16)], axis=0)
                qa = lax.dot_general(lhs, kbf, (((1,), (1,)), ((), ())), preferred_element_type=F32)
                for m, it in enumerate((h0, h1)):
                    qk[it] = (qa[:blk] * dec[it]).astype(BF16)
                    amat[it] = jnp.where(strict, qa[(m + 1) * blk:(m + 2) * blk] * dec[it], 0.0)
        yield

    def invert(nbs):
        its = [(nb, hh) for nb in nbs for hh in hh_all]
        pw = {it: amat.pop(it) for it in its}
        unpaired = {it: eye - pw[it] for it in its}
        prods = {it: [] for it in its}
        span = 2
        while span < blk:
            for it in its:
                pw[it] = mm(pw[it], pw[it])
            yield
            for it in its:
                if unpaired[it] is None:
                    unpaired[it] = eye + pw[it]
                else:
                    x = unpaired[it]
                    unpaired[it] = None
                    prods[it].append(x + mm(x, pw[it]))
            yield
            for it in its:
                if len(prods[it]) == 2:
                    prods[it] = [times(prods[it][0], prods[it][1])]
            span *= 2
        for it in its:
            fs = prods[it] + ([unpaired[it]] if unpaired[it] is not None else [])
            t = fs[0]
            for f in fs[1:]:
                t = times(t, f)
            nb, hh = it
            v = v_ref[0, rows[nb], head_cols(hh)].astype(F32)
            rhs = jnp.concatenate([kb[it] * eg[it], v * beta_c[it]], axis=1).astype(BF16)
            wu = jnp.dot(t.astype(BF16), rhs, preferred_element_type=F32)
            qd = q_ref[0, rows[nb], head_cols(hh // 2)].astype(F32) * eg[it]
            wq[it] = jnp.concatenate([wu[:, :C_HEAD_DIM], qd], axis=0).astype(BF16)
            u[it] = wu[:, C_HEAD_DIM:]
        yield

    s = [s_ref[0, hh] for hh in hh_all]

    def scan(nbs):
        for nb in nbs:
            ws_qs = [jnp.dot(wq.pop((nb, hh)), s[hh].astype(BF16), preferred_element_type=F32) for hh in hh_all]
            yield
            vnb = [(u.pop((nb, hh)) - ws_qs[hh][:blk]).astype(BF16) for hh in hh_all]
            for hh in hh_all:
                it = (nb, hh)
                k = k_ref[0, rows[nb], head_cols(hh // 2)].astype(F32)
                k_dec = (k * jnp.exp(g_last[it] - g_c[it])).astype(BF16)
                s[hh] = s[hh] * jnp.exp(g_last[it]) + lax.dot_general(
                    k_dec, vnb[hh], (((0,), (0,)), ((), ())), preferred_element_type=F32)
            yield
            for hh in hh_all:
                o = ws_qs[hh][blk:] + jnp.dot(qk.pop((nb, hh)), vnb[hh], preferred_element_type=F32)
                on = o * lax.rsqrt(jnp.mean(o * o, axis=-1, keepdims=True) + NORM_EPS) * gn
                hs = head_cols(hh)
                o_ref[0, rows[nb], hs] = (on * _silu(z_ref[0, rows[nb], hs].astype(F32))).astype(o_ref.dtype)
            yield

    def round_robin(gens):
        gens = list(gens)
        while gens:
            for g in list(gens):
                try:
                    next(g)
                except StopIteration:
                    gens.remove(g)

    gsz = min(group, nblk)
    groups = [list(range(g0, g0 + gsz)) for g0 in range(0, nblk, gsz)]
    round_robin([prepare(groups[0])])
    for gi in range(len(groups) + 1):
        active = []
        if gi < len(groups):
            active.append(invert(groups[gi]))
        if gi + 1 < len(groups):
            active.append(prepare(groups[gi + 1]))
        if gi >= 1:
            active.append(scan(groups[gi - 1]))
        round_robin(active)
    for hh in hh_all:
        s_ref[0, hh] = s[hh]


def _delta(p1, bg, s0, o_norm_g, *, heads, nblk, group=4):
    b, t, _ = p1.shape
    blk = CHUNK
    tb = nblk * blk
    cq = heads // 2 * C_HEAD_DIM
    cv = heads * C_HEAD_DIM
    ko, vo, zo = C_QK_WIDTH // cq, 2 * C_QK_WIDTH // cv, C_CONV_CH // cv

    def tok(width, off):
        return pl.BlockSpec((1, tb, width), lambda bb, hg, n: (bb, n, off + hg))

    state_spec = pl.BlockSpec((1, heads, C_HEAD_DIM, C_HEAD_DIM), lambda bb, hg, n: (bb, hg, 0, 0))
    return pl.pallas_call(
        functools.partial(_delta_kernel, heads=heads, nblk=nblk, blk=blk, group=group),
        grid=(b, C_V_HEADS // heads, t // tb),
        in_specs=[tok(cq, 0), tok(cq, ko), tok(cv, vo), tok(cv, zo),
                  pl.BlockSpec((1, tb, LANES), lambda bb, hg, n: (bb, n, 0)),
                  pl.BlockSpec((1, C_HEAD_DIM), lambda bb, hg, n: (0, 0)),
                  state_spec],
        out_specs=[pl.BlockSpec((1, tb, cv), lambda bb, hg, n: (bb, n, hg)), state_spec],
        out_shape=[jax.ShapeDtypeStruct((b, t, C_V_WIDTH), BF16),
                   jax.ShapeDtypeStruct((b, C_V_HEADS, C_HEAD_DIM, C_HEAD_DIM), F32)],
        scratch_shapes=[pltpu.VMEM((nblk, LANES, blk), F32)],
        compiler_params=_cparams(("parallel", "parallel", "arbitrary")),
        name="delta",
    )(p1, p1, p1, p1, bg, o_norm_g.reshape(1, C_HEAD_DIM), s0)


def _trunk(x, mods, wts, cache_k, cache_v, conv_left, s0, *, prompt):
    (w_in_ab, bias_blocks, rel_table, ln_v_g, ln_v_b, w_s, b_s, w_out_ab, w_in_dn, w_extra, w_conv, ad_rows,
     o_norm_g, w_out_dn, ln_g, ln_b) = wts
    b, t, _ = x.shape
    p0 = _inproj(x, mods[0], w_in_ab, tm=1024, tn=1024, cm=256)
    if prompt:
        o_a = _attn_prompt(p0, bias_blocks)
        o_b = _gmlp(p0, w_s, b_s, ln_v_g, ln_v_b, tb=256, want_vn=False)
        v_n = None
        keep = min(A_BAND * CHUNK, t)
    else:
        o_a = _attn_sample(p0, cache_k, cache_v, rel_table)
        o_b, v_n = _gmlp(p0, w_s, b_s, ln_v_g, ln_v_b, tb=256, want_vn=True)
        keep = t
    new_k = p0[:, t - keep:, A_WIDTH:2 * A_WIDTH].astype(F32).reshape(b, keep, A_HEADS, A_HEAD_DIM)
    new_v = p0[:, t - keep:, 2 * A_WIDTH:3 * A_WIDTH].astype(F32).reshape(b, keep, A_HEADS, A_HEAD_DIM)
    x1 = _outproj([o_a, o_b], w_out_ab, x, mods[0], ln_g[0], ln_b[0], tm=512)
    if conv_left is None:
        left8 = jnp.zeros((b, HALO_ROWS, C_CONV_CH), F32)
    else:
        left8 = jnp.pad(conv_left, ((0, 0), (HALO_ROWS - (C_CONV - 1), 0), (0, 0)))
    p1, bg, tail = _inproj(x1, mods[1], w_in_dn, (w_extra, ad_rows, w_conv, left8),
                           tm=1024, tn=1024, cm=256, cn=1024)
    new_conv = tail[:, -1, HALO_ROWS - (C_CONV - 1):, :]
    if s0 is None:
        s0 = jnp.zeros((b, C_V_HEADS, C_HEAD_DIM, C_HEAD_DIM), F32)
    t_pad = -t % CHUNK
    if t_pad:
        p1 = jnp.pad(p1, ((0, 0), (0, t_pad), (0, 0)))
        bg = jnp.concatenate([jnp.pad(bg[..., :C_V_HEADS], ((0, 0), (0, t_pad), (0, 0))),
                              jnp.pad(bg[..., C_V_HEADS:], ((0, 0), (0, t_pad), (0, 0)), mode="edge")], axis=-1)
    nblk = max(n for n in (4, 2, 1) if (t + t_pad) % (n * CHUNK) == 0)
    o_c, s_new = _delta(p1, bg, s0, o_norm_g, heads=8, nblk=nblk)
    if t_pad:
        o_c = o_c[:, :t]
    x2 = _outproj([o_c], w_out_dn, x1, mods[1], ln_g[1], ln_b[1], tm=512)
    return x2, new_k, new_v, v_n, new_conv, s_new


def kernel(x_prompt, x_sample, cache_a_k, cache_a_v, state_c_conv, state_c_s, c_prompt, c_sample, w_ada, b_ada,
           ln_g, ln_b, w_in_ab, rel_table, ln_v_g, ln_v_b, w_s, b_s, w_out_ab, w_in_dn, w_conv, a_log, dt_bias,
           o_norm_g, w_out_dn):
    bp = c_prompt.shape[0]
    bs = c_sample.shape[0]
    c_all = jnp.concatenate([c_prompt, c_sample], axis=0)
    c_all = jnp.pad(c_all, ((0, -(bp + bs) % 16), (0, 0)))
    ada = _ada(c_all, w_ada, b_ada)
    mods_p = [ada[l, :bp].reshape(bp, 3, D_MODEL) for l in range(DEPTH)]
    mods_s = [ada[l, bp:bp + bs].reshape(bs, 3, D_MODEL) for l in range(DEPTH)]

    extra = w_in_dn[:, C_MAIN:]
    w_extra = jnp.pad(extra, ((0, 0), (0, LANES - extra.shape[1]))).astype(BF16)
    ad_rows = jnp.zeros((2, LANES), F32)
    ad_rows = ad_rows.at[0, C_V_HEADS:2 * C_V_HEADS].set(a_log).at[1, C_V_HEADS:2 * C_V_HEADS].set(dt_bias)
    wts = (w_in_ab.astype(BF16), _band_bias_tile(rel_table), rel_table, ln_v_g, ln_v_b, w_s, b_s,
           w_out_ab.astype(BF16), w_in_dn[:, :C_MAIN].astype(BF16), w_extra, w_conv, ad_rows, o_norm_g,
           w_out_dn.astype(BF16), ln_g, ln_b)

    y_p, p_a_k, p_a_v, _, p_c_conv, p_c_s = _trunk(x_prompt, mods_p, wts, None, None, None, None, prompt=True)
    y_s, s_a_k, s_a_v, s_b_v, s_c_conv, s_c_s = _trunk(x_sample, mods_s, wts, cache_a_k, cache_a_v, state_c_conv,
                                                       state_c_s, prompt=False)
    s_b_v = s_b_v.reshape(bs, -1, B_GROUPS, B_GROUP_DIM)
    return (y_p, y_s, p_a_k, p_a_v, p_c_conv, p_c_s, s_a_k, s_a_v, s_b_v, s_c_conv, s_c_s)
```

```python
import functools

import jax
import jax.numpy as jnp
from jax import lax
from jax.experimental import pallas as pl
from jax.experimental.pallas import tpu as pltpu

F32 = jnp.float32
BF16 = jnp.bfloat16

D_MODEL = 2048
DEPTH = 2
CHUNK = 64
A_HEADS = 16
A_HEAD_DIM = 64
A_WIDTH = A_HEADS * A_HEAD_DIM
A_BAND = 8
REL_MAX = 128
B_GROUPS = 8
B_GROUP_DIM = 128
B_WIDTH = B_GROUPS * B_GROUP_DIM
B_CHUNK = 128
C_QK_HEADS = 16
C_V_HEADS = 32
C_HEAD_DIM = 128
C_QK_WIDTH = C_QK_HEADS * C_HEAD_DIM
C_V_WIDTH = C_V_HEADS * C_HEAD_DIM
C_CONV = 4
C_CONV_CH = 2 * C_QK_WIDTH + C_V_WIDTH
C_MAIN = C_CONV_CH + C_V_WIDTH
DN_ALPHA = (2 * DEPTH) ** 0.25
LN_EPS = 1e-5
NORM_EPS = 1e-6
NEG_BIG = -1e30

LANES = 128
HALO_ROWS = 8
VMEM_LIMIT = 56 * 1024 * 1024


def _cparams(sem):
    return pltpu.CompilerParams(dimension_semantics=sem, vmem_limit_bytes=VMEM_LIMIT)


def _silu(x):
    return x * jax.nn.sigmoid(x)


def _gelu(x):
    return 0.5 * x * (1.0 + lax.erf(x * (2.0 ** -0.5)))


def _layer_norm(x, g, b):
    mu = jnp.mean(x, axis=-1, keepdims=True)
    xc = x - mu
    var = jnp.mean(xc * xc, axis=-1, keepdims=True)
    return xc * lax.rsqrt(var + LN_EPS) * g + b


def _ada_kernel(c_ref, w_ref, b_ref, o_ref):
    c = c_ref[...].astype(BF16)
    w = w_ref[0].astype(BF16)
    o_ref[0] = jnp.dot(c, w, preferred_element_type=F32) + b_ref[0]


def _ada(c_all, w_ada, b_ada):
    rows, d = c_all.shape
    depth, _, n = w_ada.shape
    tn = 768
    return pl.pallas_call(
        _ada_kernel,
        grid=(depth, n // tn),
        in_specs=[
            pl.BlockSpec((rows, d), lambda l, j: (0, 0)),
            pl.BlockSpec((1, d, tn), lambda l, j: (l, 0, j)),
            pl.BlockSpec((1, 1, tn), lambda l, j: (l, 0, j)),
        ],
        out_specs=pl.BlockSpec((1, rows, tn), lambda l, j: (l, 0, j)),
        out_shape=jax.ShapeDtypeStruct((depth, rows, n), F32),
        compiler_params=_cparams(("parallel", "parallel")),
        name="ada",
    )(c_all, w_ada, b_ada.reshape(depth, 1, n))


def _inproj_kernel(*refs, bt, tm, tn, cm, cn, dn):
    if dn:
        x_ref, mod_ref, w_ref, we_ref, ad_ref, wc_ref, cl_ref, o_ref, oe_ref, tail_ref, h_scr, halo_scr = refs
    else:
        x_ref, mod_ref, w_ref, o_ref, h_scr = refs
    i = pl.program_id(1)
    j = pl.program_id(2)
    rows = bt * tm

    @pl.when(j == 0)
    def _():
        h = x_ref[...] * (1.0 + mod_ref[:, 1:2, :]) + mod_ref[:, 0:1, :]
        hb = h.reshape(rows, h.shape[-1]).astype(BF16)
        h_scr[...] = hb
        if dn:
            raw = jnp.dot(hb, we_ref[...], preferred_element_type=F32)
            beta = jax.nn.sigmoid(raw)
            xa = raw + ad_ref[1:2, :]
            softplus = jnp.maximum(xa, 0.0) + jnp.log1p(jnp.exp(-jnp.abs(xa)))
            la = -jnp.exp(ad_ref[0:1, :]) * softplus
            gb = min(CHUNK, tm)
            grp = min(rows, 4 * CHUNK)
            r = lax.broadcasted_iota(jnp.int32, (grp, grp), 0)
            c = lax.broadcasted_iota(jnp.int32, (grp, grp), 1)
            sh = gb.bit_length() - 1
            same = lax.shift_right_logical(r, sh) == lax.shift_right_logical(c, sh)
            csum = ((c <= r) & same).astype(F32)
            g = jnp.concatenate(
                [jnp.dot(csum, la[g0:g0 + grp], preferred_element_type=F32, precision=lax.Precision.HIGHEST)
                 for g0 in range(0, rows, grp)], axis=0)
            lane = lax.broadcasted_iota(jnp.int32, raw.shape, 1)
            bg = jnp.where(lane < C_V_HEADS, beta, g)
            oe_ref[...] = bg.reshape(bt, tm, bg.shape[-1])

    def plain():
        acc = jnp.dot(h_scr[...], w_ref[...], preferred_element_type=F32)
        o_ref[...] = acc.reshape(bt, tm, tn).astype(o_ref.dtype)

    if not dn:
        plain()
        return

    n_conv = C_CONV_CH // tn

    def conv_tiles(l2norm):
        qscale = jnp.where(j < C_QK_WIDTH // tn, C_HEAD_DIM ** -0.5, 1.0)
        row8 = lax.broadcasted_iota(jnp.int32, (HALO_ROWS, cn), 0)
        ncp = tn // cn

        def cols(nt):
            return slice(nt * cn, (nt + 1) * cn)

        def epilogue(seg, prev, nt):
            taps = wc_ref[:, cols(nt)]
            y = taps[C_CONV - 1:C_CONV] * seg
            for delay in range(1, C_CONV):
                rolled = pltpu.roll(seg, delay, 0)
                first = jnp.where(row8 < delay, pltpu.roll(prev, delay, 0), rolled[:HALO_ROWS])
                shifted = jnp.concatenate([first, rolled[HALO_ROWS:]], axis=0)
                y = y + taps[C_CONV - 1 - delay:C_CONV - delay] * shifted
            s = _silu(y)
            if not l2norm:
                return s.astype(o_ref.dtype)
            outs = []
            for hd in range(cn // C_HEAD_DIM):
                sg = s[:, hd * C_HEAD_DIM:(hd + 1) * C_HEAD_DIM]
                outs.append(sg * (lax.rsqrt(jnp.sum(sg * sg, axis=-1, keepdims=True) + NORM_EPS) * qscale))
            return jnp.concatenate(outs, axis=1).astype(o_ref.dtype)

        if bt == 1:
            @pl.when(i == 0)
            def _():
                halo_scr[j] = cl_ref[0]

            prev = {nt: halo_scr[j, :, cols(nt)] for nt in range(ncp)}
            pieces = [(c, nt) for c in range(tm // cm) for nt in range(ncp)]
            pending = None
            for p in range(len(pieces) + 1):
                issued = None
                if p < len(pieces):
                    c, nt = pieces[p]
                    issued = (c, nt, jnp.dot(h_scr[c * cm:(c + 1) * cm, :], w_ref[:, cols(nt)],
                                             preferred_element_type=F32))
                if pending is not None:
                    c, nt, a = pending
                    o_ref[0, c * cm:(c + 1) * cm, cols(nt)] = epilogue(a, prev[nt], nt)
                    prev[nt] = a[cm - HALO_ROWS:cm]
                pending = issued
            for nt in range(ncp):
                halo_scr[j, :, cols(nt)] = prev[nt]
                tail_ref[0, 0, :, cols(nt)] = prev[nt]
        else:
            for nt in range(ncp):
                acc = jnp.dot(h_scr[...], w_ref[:, cols(nt)], preferred_element_type=F32)
                for bi in range(bt):
                    seg = acc[bi * tm:(bi + 1) * tm]
                    o_ref[bi, :, cols(nt)] = epilogue(seg, cl_ref[bi, :, cols(nt)], nt)
                    tail_ref[bi, 0, :, cols(nt)] = seg[tm - HALO_ROWS:tm]

    n_qk = (2 * C_QK_WIDTH) // tn
    pl.when(j < n_qk)(functools.partial(conv_tiles, True))
    pl.when((j >= n_qk) & (j < n_conv))(functools.partial(conv_tiles, False))
    pl.when(j >= n_conv)(plain)


def _inproj(x, mod, w, dn_args=None, *, tm, tn, cm, cn=256):
    b, t, d = x.shape
    n = w.shape[1]
    tm = min(tm, t)
    bt = b if t == tm and b * t <= 1024 else 1
    cm = min(cm, tm)
    dn = dn_args is not None
    in_specs = [
        pl.BlockSpec((bt, tm, d), lambda bb, i, j: (bb, i, 0)),
        pl.BlockSpec((bt, 3, d), lambda bb, i, j: (bb, 0, 0)),
        pl.BlockSpec((d, tn), lambda bb, i, j: (0, j)),
    ]
    out_specs = [pl.BlockSpec((bt, tm, tn), lambda bb, i, j: (bb, i, j))]
    out_shape = [jax.ShapeDtypeStruct((b, t, n), BF16)]
    scratch = [pltpu.VMEM((bt * tm, d), BF16)]
    args = [x, mod, w]
    if dn:
        w_extra, ad_rows, w_conv, conv_left8 = dn_args
        ne = w_extra.shape[1]
        last = C_CONV_CH // tn - 1

        def cj(j):
            return jnp.minimum(j, last)

        in_specs += [
            pl.BlockSpec((d, ne), lambda bb, i, j: (0, 0)),
            pl.BlockSpec((2, ne), lambda bb, i, j: (0, 0)),
            pl.BlockSpec((C_CONV, tn), lambda bb, i, j: (0, cj(j))),
            pl.BlockSpec((bt, HALO_ROWS, tn), lambda bb, i, j: (bb, 0, cj(j))),
        ]
        out_specs += [
            pl.BlockSpec((bt, tm, ne), lambda bb, i, j: (bb, i, 0)),
            pl.BlockSpec((bt, 1, HALO_ROWS, tn), lambda bb, i, j: (bb, i, 0, cj(j))),
        ]
        out_shape += [jax.ShapeDtypeStruct((b, t, ne), F32),
                      jax.ShapeDtypeStruct((b, t // tm, HALO_ROWS, C_CONV_CH), F32)]
        scratch.append(pltpu.VMEM((C_CONV_CH // tn, HALO_ROWS, tn), F32))
        args += [w_extra, ad_rows, w_conv, conv_left8]
    res = pl.pallas_call(
        functools.partial(_inproj_kernel, bt=bt, tm=tm, tn=tn, cm=cm, cn=min(cn, tn), dn=dn),
        grid=(b // bt, t // tm, n // tn),
        in_specs=in_specs,
        out_specs=out_specs,
        out_shape=out_shape,
        scratch_shapes=scratch,
        compiler_params=_cparams(("parallel", "arbitrary", "arbitrary")),
        name="inproj_dn" if dn else "inproj",
    )(*args)
    return res if dn else res[0]


ATTN_TQ = 256
ATTN_SUB = 128
ATTN_WIN = ATTN_SUB + A_BAND * CHUNK
ATTN_HEADS = 8


def _attn_kernel(q_ref, k0_ref, k1_ref, k2_ref, v0_ref, v1_ref, v2_ref, z_ref, bias_ref, o_ref, bias_scr):
    i = pl.program_id(2)
    nsub = ATTN_TQ // ATTN_SUB

    @pl.when(i <= 2)
    def _():
        for sb in range(nsub):
            col = lax.broadcasted_iota(jnp.int32, (1, ATTN_WIN), 1) + sb * ATTN_SUB
            missing = col < (2 - i) * ATTN_TQ
            for h in range(ATTN_HEADS):
                bias_scr[sb * ATTN_HEADS + h] = jnp.where(missing, NEG_BIG, bias_ref[h])

    q = q_ref[0] * (A_HEAD_DIM ** -0.5)
    k = jnp.concatenate([k0_ref[0], k1_ref[0], k2_ref[0]], axis=0)
    v = jnp.concatenate([v0_ref[0], v1_ref[0], v2_ref[0]], axis=0)
    items = [(h, sb) for h in range(ATTN_HEADS) for sb in range(nsub)]
    low = lax.broadcasted_iota(jnp.int32, (1, LANES), 1) < A_HEAD_DIM

    def pair(h):
        return slice(h // 2 * LANES, (h // 2 + 1) * LANES)

    def own(h):
        return low if h % 2 == 0 else jnp.logical_not(low)

    q_own = [jnp.where(own(h), q[:, pair(h)], jnp.zeros((), BF16)) for h in range(ATTN_HEADS)]
    v_ext = [jnp.where(own(h), v[:, pair(h)], jnp.ones((), BF16)) for h in range(ATTN_HEADS)]

    def scores(it):
        h, sb = it
        qs = q_own[h][sb * ATTN_SUB:(sb + 1) * ATTN_SUB]
        ks = k[sb * ATTN_SUB:sb * ATTN_SUB + ATTN_WIN, pair(h)]
        s = lax.dot_general(qs, ks, (((1,), (1,)), ((), ())), preferred_element_type=F32)
        return s + bias_scr[sb * ATTN_HEADS + h]

    def weights(s):
        return jnp.exp(s - jnp.max(s, axis=-1, keepdims=True)).astype(BF16)

    def attend(it, p):
        h, sb = it
        return jnp.dot(p, v_ext[h][sb * ATTN_SUB:sb * ATTN_SUB + ATTN_WIN], preferred_element_type=F32)

    ss = [scores(it) for it in items]
    ps = [weights(s) for s in ss]
    ol = {it: attend(it, p) for it, p in zip(items, ps)}

    def normalised(h0, sb):
        e0, e1 = ol[(h0, sb)], ol[(h0 + 1, sb)]
        return jnp.where(low, e0 / pltpu.roll(e0, A_HEAD_DIM, 1), e1 / pltpu.roll(e1, A_HEAD_DIM, 1))

    o = jnp.concatenate(
        [jnp.concatenate([normalised(h0, sb) for sb in range(nsub)], axis=0) for h0 in range(0, ATTN_HEADS, 2)],
        axis=1)
    o_ref[0] = (o * _silu(z_ref[0].astype(F32))).astype(o_ref.dtype)


def _band_bias_tile(rel_table):
    h = rel_table.shape[0]
    far = A_BAND * CHUNK
    n_diag = ATTN_WIN + ATTN_SUB - 1
    n_const = far + ATTN_SUB - 1 - REL_MAX + 1
    w = jnp.concatenate([jnp.broadcast_to(rel_table[:, 2 * REL_MAX:], (h, n_const)),
                         rel_table[:, 2 * REL_MAX - 1::-1][:, :n_diag - n_const]], axis=1).astype(F32)
    tiled =jnp.tile(jnp.pad(w, ((0, 0), (0, 1))), (1, ATTN_SUB))[:, :ATTN_SUB * n_diag]
    toep = tiled.reshape(h, ATTN_SUB, n_diag)[:, :, ATTN_SUB - 1:ATTN_SUB - 1 + ATTN_WIN]
    row = jnp.arange(ATTN_SUB)[:, None]
    colx = jnp.arange(ATTN_WIN)[None, :]
    rel = colx - (row // CHUNK) * CHUNK
    band = (rel >= 0) & (rel < (A_BAND + 1) * CHUNK)
    return jnp.where(band[None], toep, NEG_BIG)


def _attn_prompt(p0, bias):
    b, t, _ = p0.shape
    tq = ATTN_TQ
    width = ATTN_HEADS * A_HEAD_DIM
    per = A_WIDTH // width
    qo, ko, vo, zo = 0, per, 2 * per, 3 * per

    def blk(off, back):
        return pl.BlockSpec((1, tq, width), lambda hg, bb, i: (bb, jnp.maximum(i - back, 0), off + hg))

    return pl.pallas_call(
        _attn_kernel,
        grid=(per, b, t // tq),
        in_specs=[blk(qo, 0), blk(ko, 2), blk(ko, 1), blk(ko, 0), blk(vo, 2), blk(vo, 1), blk(vo, 0), blk(zo, 0),
                  pl.BlockSpec((ATTN_HEADS, ATTN_SUB, ATTN_WIN), lambda hg, bb, i: (hg, 0, 0))],
        out_specs=pl.BlockSpec((1, tq, width), lambda hg, bb, i: (bb, i, hg)),
        out_shape=jax.ShapeDtypeStruct((b, t, A_WIDTH), BF16),
        scratch_shapes=[pltpu.VMEM((ATTN_TQ // ATTN_SUB * ATTN_HEADS, ATTN_SUB, ATTN_WIN), F32)],
        compiler_params=_cparams(("parallel", "parallel", "arbitrary")),
        name="attn_prompt",
    )(p0, p0, p0, p0, p0, p0, p0, p0, bias)


def _attn_sample_kernel(q_ref, k_ref, v_ref, z_ref, ck_ref, cv_ref, bias_ref, o_ref, *, pad_rows):
    q = q_ref[0]
    zpad = jnp.zeros((pad_rows, A_WIDTH), BF16)
    k = jnp.concatenate([ck_ref[0], k_ref[0], zpad], axis=0)
    v = jnp.concatenate([cv_ref[0], v_ref[0], zpad], axis=0)
    def head(h):
        return slice(h * A_HEAD_DIM, (h + 1) * A_HEAD_DIM)

    ss = [lax.dot_general(q[:, head(h)], k[:, head(h)], (((1,), (1,)), ((), ())), preferred_element_type=F32)
          * (A_HEAD_DIM ** -0.5) + bias_ref[h] for h in range(A_HEADS)]
    ps = [jnp.exp(s - jnp.max(s, axis=-1, keepdims=True)) for s in ss]
    outs = [jnp.dot(p.astype(BF16), v[:, head(h)], preferred_element_type=F32) / jnp.sum(p, axis=-1, keepdims=True)
            for h, p in enumerate(ps)]
    o = jnp.concatenate(outs, axis=1)
    o_ref[0] = (o * _silu(z_ref[0].astype(F32))).astype(o_ref.dtype)


def _attn_sample(p0, cache_k, cache_v, rel_table):
    b, t, _ = p0.shape
    n_cache = cache_k.shape[1]
    n_keys = n_cache + t
    n_pad = -n_keys % LANES
    dist = jnp.arange(t)[:, None] + n_cache - jnp.arange(n_keys)[None, :]
    bias = rel_table[:, jnp.clip(dist, -REL_MAX, REL_MAX) + REL_MAX].astype(F32)
    bias = jnp.pad(bias, ((0, 0), (0, 0), (0, n_pad)), constant_values=NEG_BIG)
    ck = cache_k.reshape(b, n_cache, A_WIDTH).astype(BF16)
    cv = cache_v.reshape(b, n_cache, A_WIDTH).astype(BF16)

    def col(j):
        return pl.BlockSpec((1, t, A_WIDTH), lambda bb: (bb, 0, j))

    cache_spec = pl.BlockSpec((1, n_cache, A_WIDTH), lambda bb: (bb, 0, 0))
    return pl.pallas_call(
        functools.partial(_attn_sample_kernel, pad_rows=n_pad),
        grid=(b,),
        in_specs=[col(0), col(1), col(2), col(3), cache_spec, cache_spec,
                  pl.BlockSpec((A_HEADS, t, n_keys + n_pad), lambda bb: (0, 0, 0))],
        out_specs=pl.BlockSpec((1, t, A_WIDTH), lambda bb: (bb, 0, 0)),
        out_shape=jax.ShapeDtypeStruct((b, t, A_WIDTH), BF16),
        compiler_params=_cparams(("parallel",)),
        name="attn_sample",
    )(p0, p0, p0, p0, ck, cv, bias)


def _gmlp_kernel(u_ref, v_ref, z_ref, ws_ref, bst_ref, g_ref, b_ref, o_ref, *vn_out, n_mix, tb):
    vn = _layer_norm(_gelu(v_ref[0].astype(F32)), g_ref[...], b_ref[...])
    if vn_out:
        vn_out[0][0] = vn
    z = z_ref[0].astype(F32)
    gate = _gelu(u_ref[0].astype(F32)) * _silu(z)
    row = lax.broadcasted_iota(jnp.int32, (B_CHUNK, B_CHUNK), 0)
    colm = lax.broadcasted_iota(jnp.int32, (B_CHUNK, B_CHUNK), 1)
    keep = (colm <= row) & (row < n_mix)
    rows = min(tb, B_CHUNK)
    for g in range(B_GROUPS):
        gs = slice(g * B_GROUP_DIM, (g + 1) * B_GROUP_DIM)
        w = jnp.where(keep, ws_ref[g], 0.0).astype(BF16)
        bias = bst_ref[:, g:g + 1]
        for c in range(max(tb // B_CHUNK, 1)):
            rs = slice(c * B_CHUNK, c * B_CHUNK + rows)
            vg = vn[rs, gs].astype(BF16)
            if rows < B_CHUNK:
                vg = jnp.concatenate([vg, jnp.zeros((B_CHUNK - rows, B_GROUP_DIM), BF16)], axis=0)
            mix = jnp.dot(w, vg, preferred_element_type=F32) + bias
            o_ref[0, rs, gs] = (gate[rs, gs] * mix[:rows]).astype(o_ref.dtype)


def _gmlp(p0, w_s, b_s, ln_v_g, ln_v_b, *, tb, want_vn):
    b, t, _ = p0.shape
    tb = min(tb, t)
    n_mix = min(t, B_CHUNK)
    base = 4 * A_WIDTH // B_WIDTH

    def col(j):
        return pl.BlockSpec((1, tb, B_WIDTH), lambda bb, i: (bb, i, base + j))

    out_specs = [pl.BlockSpec((1, tb, B_WIDTH), lambda bb, i: (bb, i, 0))]
    out_shape = [jax.ShapeDtypeStruct((b, t, B_WIDTH), BF16)]
    if want_vn:
        out_specs.append(pl.BlockSpec((1, tb, B_WIDTH), lambda bb, i: (bb, i, 0)))
        out_shape.append(jax.ShapeDtypeStruct((b, t, B_WIDTH), F32))
    res = pl.pallas_call(
        functools.partial(_gmlp_kernel, n_mix=n_mix, tb=tb),
        grid=(b, t // tb),
        in_specs=[col(0), col(1), col(2),
                  pl.BlockSpec((B_GROUPS, B_CHUNK, B_CHUNK), lambda bb, i: (0, 0, 0)),
                  pl.BlockSpec((B_CHUNK, B_GROUPS), lambda bb, i: (0, 0)),
                  pl.BlockSpec((1, B_WIDTH), lambda bb, i: (0, 0)),
                  pl.BlockSpec((1, B_WIDTH), lambda bb, i: (0, 0))],
        out_specs=out_specs,
        out_shape=out_shape,
        compiler_params=_cparams(("parallel", "parallel")),
        name="gmlp",
    )(p0, p0, p0, w_s, b_s.T, ln_v_g.reshape(1, -1), ln_v_b.reshape(1, -1))
    return res if want_vn else res[0]


def _outproj_kernel(*refs, widths):
    o_refs = refs[:len(widths)]
    w_ref, x_ref, mod_ref, g_ref, b_ref, out_ref = refs[len(widths):]
    o = o_refs[0][0] if len(widths) == 1 else jnp.concatenate([o_ref[0] for o_ref in o_refs], axis=1)
    y = jnp.dot(o, w_ref[...], preferred_element_type=F32)
    r = DN_ALPHA * x_ref[0] + (1.0 + mod_ref[0, 2:3, :]) * y
    out_ref[0] = _layer_norm(r, g_ref[...], b_ref[...])


def _outproj(os_, w, x, mod, ln_g, ln_b, *, tm):
    b, t, d = x.shape
    tm = min(tm, t)
    widths = tuple(o.shape[-1] for o in os_)
    ktot = sum(widths)
    in_specs = [pl.BlockSpec((1, tm, kw), lambda bb, i: (bb, i, 0)) for kw in widths]
    in_specs += [
        pl.BlockSpec((ktot, d), lambda bb, i: (0, 0), pipeline_mode=pl.Buffered(1)),
        pl.BlockSpec((1, tm, d), lambda bb, i: (bb, i, 0)),
        pl.BlockSpec((1, 3, d), lambda bb, i: (bb, 0, 0)),
        pl.BlockSpec((1, d), lambda bb, i: (0, 0)),
        pl.BlockSpec((1, d), lambda bb, i: (0, 0)),
    ]
    return pl.pallas_call(
        functools.partial(_outproj_kernel, widths=widths),
        grid=(b, t // tm),
        in_specs=in_specs,
        out_specs=pl.BlockSpec((1, tm, d), lambda bb, i: (bb, i, 0)),
        out_shape=jax.ShapeDtypeStruct((b, t, d), F32),
        compiler_params=_cparams(("parallel", "parallel")),
        name="outproj",
    )(*os_, w, x, mod, ln_g.reshape(1, d), ln_b.reshape(1, d))


def _delta_kernel(q_ref, k_ref, v_ref, z_ref, bg_ref, gn_ref, s0_ref, o_ref, s_ref, gt_scr,
                  *, heads, nblk, blk, group):
    hg = pl.program_id(1)
    n = pl.program_id(2)

    @pl.when(n == 0)
    def _():
        s_ref[...] = s0_ref[...]

    def head_cols(j):
        return slice(j * C_HEAD_DIM, (j + 1) * C_HEAD_DIM)

    lane = lax.broadcasted_iota(jnp.int32, (blk, LANES), 1)
    ri = lax.broadcasted_iota(jnp.int32, (blk, blk), 0)
    ci = lax.broadcasted_iota(jnp.int32, (blk, blk), 1)
    incl = ci <= ri
    strict = ci < ri
    eye = (ci == ri).astype(F32)
    gn = gn_ref[...]

    rows = [slice(nb * blk, (nb + 1) * blk) for nb in range(nblk)]
    hh_all = range(heads)
    beta_c, g_c, g_last, eg, kb, qk, amat, wq, u = {}, {}, {}, {}, {}, {}, {}, {}, {}

    def mm(a, b):
        return jnp.dot(a.astype(BF16), b.astype(BF16), preferred_element_type=F32)

    def times(a, b):
        return a + mm(a, b - eye)

    def prepare(nbs):
        dec = {}
        for nb in nbs:
            bg = bg_ref[0, rows[nb], :]
            gt_scr[nb] = bg.T
            for hh in hh_all:
                it = (nb, hh)
                hidx = hg * heads + hh
                beta_c[it] = jnp.sum(jnp.where(lane == hidx, bg, 0.0), axis=1, keepdims=True)
                g_c[it] = jnp.sum(jnp.where(lane == C_V_HEADS + hidx, bg, 0.0), axis=1, keepdims=True)
                g_last[it] = g_c[it][blk - 1:blk, :]
                eg[it] = jnp.exp(g_c[it])
        yield
        for nb in nbs:
            for hh in hh_all:
                g_r = gt_scr[nb, pl.ds(C_V_HEADS + hg * heads + hh, 1), :]
                dec[(nb, hh)] = jnp.where(incl, jnp.exp(g_c[(nb, hh)] - g_r), 0.0)
        yield
        for nb in nbs:
            for j in range(heads // 2):
                h0, h1 = (nb, 2 * j), (nb, 2 * j + 1)
                kbf = k_ref[0, rows[nb], head_cols(j)]
                k = kbf.astype(F32)
                kb[h0] = k * beta_c[h0]
                kb[h1] = k * beta_c[h1]
                lhs = jnp.concatenate(
                    [q_ref[0, rows[nb], head_cols(j)], kb[h0].astype(BF16), kb[h1].astype(BF16)], axis=0)
                qa = lax.dot_general(lhs, kbf, (((1,), (1,)), ((), ())), preferred_element_type=F32)
                for m, it in enumerate((h0, h1)):
                    qk[it] = (qa[:blk] * dec[it]).astype(BF16)
                    amat[it] = jnp.where(strict, qa[(m + 1) * blk:(m + 2) * blk] * dec[it], 0.0)
        yield

    def invert(nbs):
        its = [(nb, hh) for nb in nbs for hh in hh_all]
        pw = {it: amat.pop(it) for it in its}
        unpaired = {it: eye - pw[it] for it in its}
        prods = {it: [] for it in its}
        span = 2
        while span < blk:
            for it in its:
                pw[it] = mm(pw[it], pw[it])
            yield
            for it in its:
                if unpaired[it] is None:
                    unpaired[it] = eye + pw[it]
                else:
                    x = unpaired[it]
                    unpaired[it] = None
                    prods[it].append(x + mm(x, pw[it]))
            yield
            for it in its:
                if len(prods[it]) == 2:
                    prods[it] = [times(prods[it][0], prods[it][1])]
            span *= 2
        for it in its:
            fs = prods[it] + ([unpaired[it]] if unpaired[it] is not None else [])
            t = fs[0]
            for f in fs[1:]:
                t = times(t, f)
            nb, hh = it
            v = v_ref[0, rows[nb], head_cols(hh)].astype(F32)
            rhs = jnp.concatenate([kb[it] * eg[it], v * beta_c[it]], axis=1).astype(BF16)
            wu = jnp.dot(t.astype(BF16), rhs, preferred_element_type=F32)
            qd = q_ref[0, rows[nb], head_cols(hh // 2)].astype(F32) * eg[it]
            wq[it] = jnp.concatenate([wu[:, :C_HEAD_DIM], qd], axis=0).astype(BF16)
            u[it] = wu[:, C_HEAD_DIM:]
        yield

    s = [s_ref[0, hh] for hh in hh_all]

    def scan(nbs):
        for nb in nbs:
            ws_qs = [jnp.dot(wq.pop((nb, hh)), s[hh].astype(BF16), preferred_element_type=F32) for hh in hh_all]
            yield
            vnb = [(u.pop((nb, hh)) - ws_qs[hh][:blk]).astype(BF16) for hh in hh_all]
            for hh in hh_all:
                it = (nb, hh)
                k = k_ref[0, rows[nb], head_cols(hh // 2)].astype(F32)
                k_dec = (k * jnp.exp(g_last[it] - g_c[it])).astype(BF16)
                s[hh] = s[hh] * jnp.exp(g_last[it]) + lax.dot_general(
                    k_dec, vnb[hh], (((0,), (0,)), ((), ())), preferred_element_type=F32)
            yield
            for hh in hh_all:
                o = ws_qs[hh][blk:] + jnp.dot(qk.pop((nb, hh)), vnb[hh], preferred_element_type=F32)
                on = o * lax.rsqrt(jnp.mean(o * o, axis=-1, keepdims=True) + NORM_EPS) * gn
                hs = head_cols(hh)
                o_ref[0, rows[nb], hs] = (on * _silu(z_ref[0, rows[nb], hs].astype(F32))).astype(o_ref.dtype)
            yield

    def round_robin(gens):
        gens = list(gens)
        while gens:
            for g in list(gens):
                try:
                    next(g)
                except StopIteration:
                    gens.remove(g)

    gsz = min(group, nblk)
    groups = [list(range(g0, g0 + gsz)) for g0 in range(0, nblk, gsz)]
    round_robin([prepare(groups[0])])
    for gi in range(len(groups) + 1):
        active = []
        if gi < len(groups):
            active.append(invert(groups[gi]))
        if gi + 1 < len(groups):
            active.append(prepare(groups[gi + 1]))
        if gi >= 1:
            active.append(scan(groups[gi - 1]))
        round_robin(active)
    for hh in hh_all:
        s_ref[0, hh] = s[hh]


def _delta(p1, bg, s0, o_norm_g, *, heads, nblk, group=4):
    b, t, _ = p1.shape
    blk = CHUNK
    tb = nblk * blk
    cq = heads // 2 * C_HEAD_DIM
    cv = heads * C_HEAD_DIM
    ko, vo, zo = C_QK_WIDTH // cq, 2 * C_QK_WIDTH // cv, C_CONV_CH // cv

    def tok(width, off):
        return pl.BlockSpec((1, tb, width), lambda bb, hg, n: (bb, n, off + hg))

    state_spec = pl.BlockSpec((1, heads, C_HEAD_DIM, C_HEAD_DIM), lambda bb, hg, n: (bb, hg, 0, 0))
    return pl.pallas_call(
        functools.partial(_delta_kernel, heads=heads, nblk=nblk, blk=blk, group=group),
        grid=(b, C_V_HEADS // heads, t // tb),
        in_specs=[tok(cq, 0), tok(cq, ko), tok(cv, vo), tok(cv, zo),
                  pl.BlockSpec((1, tb, LANES), lambda bb, hg, n: (bb, n, 0)),
                  pl.BlockSpec((1, C_HEAD_DIM), lambda bb, hg, n: (0, 0)),
                  state_spec],
        out_specs=[pl.BlockSpec((1, tb, cv), lambda bb, hg, n: (bb, n, hg)), state_spec],
        out_shape=[jax.ShapeDtypeStruct((b, t, C_V_WIDTH), BF16),
                   jax.ShapeDtypeStruct((b, C_V_HEADS, C_HEAD_DIM, C_HEAD_DIM), F32)],
        scratch_shapes=[pltpu.VMEM((nblk, LANES, blk), F32)],
        compiler_params=_cparams(("parallel", "parallel", "arbitrary")),
        name="delta",
    )(p1, p1, p1, p1, bg, o_norm_g.reshape(1, C_HEAD_DIM), s0)


def _trunk(x, mods, wts, cache_k, cache_v, conv_left, s0, *, prompt):
    (w_in_ab, bias_blocks, rel_table, ln_v_g, ln_v_b, w_s, b_s, w_out_ab, w_in_dn, w_extra, w_conv, ad_rows,
     o_norm_g, w_out_dn, ln_g, ln_b) = wts
    b, t, _ = x.shape
    p0 = _inproj(x, mods[0], w_in_ab, tm=1024, tn=1024, cm=256)
    if prompt:
        o_a = _attn_prompt(p0, bias_blocks)
        o_b = _gmlp(p0, w_s, b_s, ln_v_g, ln_v_b, tb=512, want_vn=False)
        v_n = None
        keep = min(A_BAND * CHUNK, t)
    else:
        o_a = _attn_sample(p0, cache_k, cache_v, rel_table)
        o_b, v_n = _gmlp(p0, w_s, b_s, ln_v_g, ln_v_b, tb=256, want_vn=True)
        keep = t
    new_k = p0[:, t - keep:, A_WIDTH:2 * A_WIDTH].astype(F32).reshape(b, keep, A_HEADS, A_HEAD_DIM)
    new_v = p0[:, t - keep:, 2 * A_WIDTH:3 * A_WIDTH].astype(F32).reshape(b, keep, A_HEADS, A_HEAD_DIM)
    x1 = _outproj([o_a, o_b], w_out_ab, x, mods[0], ln_g[0], ln_b[0], tm=512)
    if conv_left is None:
        left8 = jnp.zeros((b, HALO_ROWS, C_CONV_CH), F32)
    else:
        left8 = jnp.pad(conv_left, ((0, 0), (HALO_ROWS - (C_CONV - 1), 0), (0, 0)))
    p1, bg, tail = _inproj(x1, mods[1], w_in_dn, (w_extra, ad_rows, w_conv, left8),
                           tm=1024, tn=1024, cm=256, cn=1024)
    new_conv = tail[:, -1, HALO_ROWS - (C_CONV - 1):, :]
    if s0 is None:
        s0 = jnp.zeros((b, C_V_HEADS, C_HEAD_DIM, C_HEAD_DIM), F32)
    t_pad = -t % CHUNK
    if t_pad:
        p1 = jnp.pad(p1, ((0, 0), (0, t_pad), (0, 0)))
        bg = jnp.concatenate([jnp.pad(bg[..., :C_V_HEADS], ((0, 0), (0, t_pad), (0, 0))),
                              jnp.pad(bg[..., C_V_HEADS:], ((0, 0), (0, t_pad), (0, 0)), mode="edge")], axis=-1)
    nblk = max(n for n in (4, 2, 1) if (t + t_pad) % (n * CHUNK) == 0)
    o_c, s_new = _delta(p1, bg, s0, o_norm_g, heads=min(C_V_HEADS, 64 // nblk), nblk=nblk)
    if t_pad:
        o_c = o_c[:, :t]
    x2 = _outproj([o_c], w_out_dn, x1, mods[1], ln_g[1], ln_b[1], tm=512)
    return x2, new_k, new_v, v_n, new_conv, s_new


def kernel(x_prompt, x_sample, cache_a_k, cache_a_v, state_c_conv, state_c_s, c_prompt, c_sample, w_ada, b_ada,
           ln_g, ln_b, w_in_ab, rel_table, ln_v_g, ln_v_b, w_s, b_s, w_out_ab, w_in_dn, w_conv, a_log, dt_bias,
           o_norm_g, w_out_dn):
    bp = c_prompt.shape[0]
    bs = c_sample.shape[0]
    c_all = jnp.concatenate([c_prompt, c_sample], axis=0)
    c_all = jnp.pad(c_all, ((0, -(bp + bs) % 16), (0, 0)))
    ada = _ada(c_all, w_ada, b_ada)
    mods_p = [ada[l, :bp].reshape(bp, 3, D_MODEL) for l in range(DEPTH)]
    mods_s = [ada[l, bp:bp + bs].reshape(bs, 3, D_MODEL) for l in range(DEPTH)]

    extra = w_in_dn[:, C_MAIN:]
    w_extra = jnp.pad(extra, ((0, 0), (0, LANES - extra.shape[1]))).astype(BF16)
    ad_rows = jnp.zeros((2, LANES), F32)
    ad_rows = ad_rows.at[0, C_V_HEADS:2 * C_V_HEADS].set(a_log).at[1, C_V_HEADS:2 * C_V_HEADS].set(dt_bias)
    wts = (w_in_ab.astype(BF16), _band_bias_tile(rel_table), rel_table, ln_v_g, ln_v_b, w_s, b_s,
           w_out_ab.astype(BF16), w_in_dn[:, :C_MAIN].astype(BF16), w_extra, w_conv, ad_rows, o_norm_g,
           w_out_dn.astype(BF16), ln_g, ln_b)

    y_p, p_a_k, p_a_v, _, p_c_conv, p_c_s = _trunk(x_prompt, mods_p, wts, None, None, None, None, prompt=True)
    y_s, s_a_k, s_a_v, s_b_v, s_c_conv, s_c_s = _trunk(x_sample, mods_s, wts, cache_a_k, cache_a_v, state_c_conv,
                                                       state_c_s, prompt=False)
    s_b_v = s_b_v.reshape(bs, -1, B_GROUPS, B_GROUP_DIM)
    return (y_p, y_s, p_a_k, p_a_v, p_c_conv, p_c_s, s_a_k, s_a_v, s_b_v, s_c_conv, s_c_s)
```

```python
import functools

import jax
import jax.numpy as jnp
from jax import lax
from jax.experimental import pallas as pl
from jax.experimental.pallas import tpu as pltpu

F32 = jnp.float32
BF16 = jnp.bfloat16

D_MODEL = 2048
DEPTH = 2
CHUNK = 64
A_HEADS = 16
A_HEAD_DIM = 64
A_WIDTH = A_HEADS * A_HEAD_DIM
A_BAND = 8
REL_MAX = 128
B_GROUPS = 8
B_GROUP_DIM = 128
B_WIDTH = B_GROUPS * B_GROUP_DIM
B_CHUNK = 128
C_QK_HEADS = 16
C_V_HEADS = 32
C_HEAD_DIM = 128
C_QK_WIDTH = C_QK_HEADS * C_HEAD_DIM
C_V_WIDTH = C_V_HEADS * C_HEAD_DIM
C_CONV = 4
C_CONV_CH = 2 * C_QK_WIDTH + C_V_WIDTH
C_MAIN = C_CONV_CH + C_V_WIDTH
DN_ALPHA = (2 * DEPTH) ** 0.25
LN_EPS = 1e-5
NORM_EPS = 1e-6
NEG_BIG = -1e30

LANES = 128
HALO_ROWS = 8
VMEM_LIMIT = 56 * 1024 * 1024


def _cparams(sem):
    return pltpu.CompilerParams(dimension_semantics=sem, vmem_limit_bytes=VMEM_LIMIT)


def _silu(x):
    return x * jax.nn.sigmoid(x)


def _gelu(x):
    return 0.5 * x * (1.0 + lax.erf(x * (2.0 ** -0.5)))


def _layer_norm(x, g, b):
    mu = jnp.mean(x, axis=-1, keepdims=True)
    xc = x - mu
    var = jnp.mean(xc * xc, axis=-1, keepdims=True)
    return xc * lax.rsqrt(var + LN_EPS) * g + b


def _ada_kernel(c_ref, w_ref, b_ref, o_ref):
    c = c_ref[...].astype(BF16)
    w = w_ref[0].astype(BF16)
    o_ref[0] = jnp.dot(c, w, preferred_element_type=F32) + b_ref[0]


def _ada(c_all, w_ada, b_ada):
    rows, d = c_all.shape
    depth, _, n = w_ada.shape
    tn = 768
    return pl.pallas_call(
        _ada_kernel,
        grid=(depth, n // tn),
        in_specs=[
            pl.BlockSpec((rows, d), lambda l, j: (0, 0)),
            pl.BlockSpec((1, d, tn), lambda l, j: (l, 0, j)),
            pl.BlockSpec((1, 1, tn), lambda l, j: (l, 0, j)),
        ],
        out_specs=pl.BlockSpec((1, rows, tn), lambda l, j: (l, 0, j)),
        out_shape=jax.ShapeDtypeStruct((depth, rows, n), F32),
        compiler_params=_cparams(("parallel", "parallel")),
        name="ada",
    )(c_all, w_ada, b_ada.reshape(depth, 1, n))


def _inproj_kernel(*refs, bt, tm, tn, cm, cn, dn):
    if dn:
        x_ref, mod_ref, w_ref, we_ref, ad_ref, wc_ref, cl_ref, o_ref, oe_ref, tail_ref, h_scr, halo_scr = refs
    else:
        x_ref, mod_ref, w_ref, o_ref, h_scr = refs
    i = pl.program_id(1)
    j = pl.program_id(2)
    rows = bt * tm

    @pl.when(j == 0)
    def _():
        h = x_ref[...] * (1.0 + mod_ref[:, 1:2, :]) + mod_ref[:, 0:1, :]
        hb = h.reshape(rows, h.shape[-1]).astype(BF16)
        h_scr[...] = hb
        if dn:
            raw = jnp.dot(hb, we_ref[...], preferred_element_type=F32)
            beta = jax.nn.sigmoid(raw)
            xa = raw + ad_ref[1:2, :]
            softplus = jnp.maximum(xa, 0.0) + jnp.log1p(jnp.exp(-jnp.abs(xa)))
            la = -jnp.exp(ad_ref[0:1, :]) * softplus
            gb = min(CHUNK, tm)
            grp = min(rows, 4 * CHUNK)
            r = lax.broadcasted_iota(jnp.int32, (grp, grp), 0)
            c = lax.broadcasted_iota(jnp.int32, (grp, grp), 1)
            sh = gb.bit_length() - 1
            same = lax.shift_right_logical(r, sh) == lax.shift_right_logical(c, sh)
            csum = ((c <= r) & same).astype(BF16)
            hi = la.astype(BF16)
            rem = la - hi.astype(F32)
            mid = rem.astype(BF16)
            lo = (rem - mid.astype(F32)).astype(BF16)
            g = jnp.concatenate(
                [sum(jnp.dot(csum, part[g0:g0 + grp], preferred_element_type=F32) for part in (hi, mid, lo))
                 for g0 in range(0, rows, grp)], axis=0)
            lane = lax.broadcasted_iota(jnp.int32, raw.shape, 1)
            bg = jnp.where(lane < C_V_HEADS, beta, g)
            oe_ref[...] = bg.reshape(bt, tm, bg.shape[-1])

    def plain():
        acc = jnp.dot(h_scr[...], w_ref[...], preferred_element_type=F32)
        o_ref[...] = acc.reshape(bt, tm, tn).astype(o_ref.dtype)

    if not dn:
        plain()
        return

    n_conv = C_CONV_CH // tn

    def conv_tiles(l2norm):
        qscale = jnp.where(j < C_QK_WIDTH // tn, C_HEAD_DIM ** -0.5, 1.0)
        row8 = lax.broadcasted_iota(jnp.int32, (HALO_ROWS, cn), 0)
        ncp = tn // cn

        def cols(nt):
            return slice(nt * cn, (nt + 1) * cn)

        def epilogue(seg, prev, nt):
            taps = wc_ref[:, cols(nt)]
            y = taps[C_CONV - 1:C_CONV] * seg
            for delay in range(1, C_CONV):
                rolled = pltpu.roll(seg, delay, 0)
                first = jnp.where(row8 < delay, pltpu.roll(prev, delay, 0), rolled[:HALO_ROWS])
                shifted = jnp.concatenate([first, rolled[HALO_ROWS:]], axis=0)
                y = y + taps[C_CONV - 1 - delay:C_CONV - delay] * shifted
            s = _silu(y)
            if not l2norm:
                return s.astype(o_ref.dtype)
            outs = []
            for hd in range(cn // C_HEAD_DIM):
                sg = s[:, hd * C_HEAD_DIM:(hd + 1) * C_HEAD_DIM]
                outs.append(sg * (lax.rsqrt(jnp.sum(sg * sg, axis=-1, keepdims=True) + NORM_EPS) * qscale))
            return jnp.concatenate(outs, axis=1).astype(o_ref.dtype)

        if bt == 1:
            @pl.when(i == 0)
            def _():
                halo_scr[j] = cl_ref[0]

            prev = {nt: halo_scr[j, :, cols(nt)] for nt in range(ncp)}
            pieces = [(c, nt) for c in range(tm // cm) for nt in range(ncp)]
            pending = None
            for p in range(len(pieces) + 1):
                issued = None
                if p < len(pieces):
                    c, nt = pieces[p]
                    issued = (c, nt, jnp.dot(h_scr[c * cm:(c + 1) * cm, :], w_ref[:, cols(nt)],
                                             preferred_element_type=F32))
                if pending is not None:
                    c, nt, a = pending
                    o_ref[0, c * cm:(c + 1) * cm, cols(nt)] = epilogue(a, prev[nt], nt)
                    prev[nt] = a[cm - HALO_ROWS:cm]
                pending = issued
            for nt in range(ncp):
                halo_scr[j, :, cols(nt)] = prev[nt]
                tail_ref[0, 0, :, cols(nt)] = prev[nt]
        else:
            for nt in range(ncp):
                acc = jnp.dot(h_scr[...], w_ref[:, cols(nt)], preferred_element_type=F32)
                for bi in range(bt):
                    seg = acc[bi * tm:(bi + 1) * tm]
                    o_ref[bi, :, cols(nt)] = epilogue(seg, cl_ref[bi, :, cols(nt)], nt)
                    tail_ref[bi, 0, :, cols(nt)] = seg[tm - HALO_ROWS:tm]

    n_qk = (2 * C_QK_WIDTH) // tn
    pl.when(j < n_qk)(functools.partial(conv_tiles, True))
    pl.when((j >= n_qk) & (j < n_conv))(functools.partial(conv_tiles, False))
    pl.when(j >= n_conv)(plain)


def _inproj(x, mod, w, dn_args=None, *, tm, tn, cm, cn=256, n_out=None):
    b, t, d = x.shape
    n = n_out or w.shape[1]
    tm = min(tm, t)
    bt = b if t == tm and b * t <= 1024 else 1
    cm = min(cm, tm)
    dn = dn_args is not None
    in_specs = [
        pl.BlockSpec((bt, tm, d), lambda bb, i, j: (bb, i, 0)),
        pl.BlockSpec((bt, 3, d), lambda bb, i, j: (bb, 0, 0)),
        pl.BlockSpec((d, tn), lambda bb, i, j: (0, j)),
    ]
    out_specs = [pl.BlockSpec((bt, tm, tn), lambda bb, i, j: (bb, i, j))]
    out_shape = [jax.ShapeDtypeStruct((b, t, n), BF16)]
    scratch = [pltpu.VMEM((bt * tm, d), BF16)]
    args = [x, mod, w]
    if dn:
        w_extra, ad_rows, w_conv, conv_left8 = dn_args
        ne = w_extra.shape[1]
        last = C_CONV_CH // tn - 1

        def cj(j):
            return jnp.minimum(j, last)

        in_specs += [
            pl.BlockSpec((d, ne), lambda bb, i, j: (0, 0)),
            pl.BlockSpec((2, ne), lambda bb, i, j: (0, 0)),
            pl.BlockSpec((C_CONV, tn), lambda bb, i, j: (0, cj(j))),
            pl.BlockSpec((bt, HALO_ROWS, tn), lambda bb, i, j: (bb, 0, cj(j))),
        ]
        out_specs += [
            pl.BlockSpec((bt, tm, ne), lambda bb, i, j: (bb, i, 0)),
            pl.BlockSpec((bt, 1, HALO_ROWS, tn), lambda bb, i, j: (bb, i, 0, cj(j))),
        ]
        out_shape += [jax.ShapeDtypeStruct((b, t, ne), F32),
                      jax.ShapeDtypeStruct((b, t // tm, HALO_ROWS, C_CONV_CH), F32)]
        scratch.append(pltpu.VMEM((C_CONV_CH // tn, HALO_ROWS, tn), F32))
        args += [w_extra, ad_rows, w_conv, conv_left8]
    res = pl.pallas_call(
        functools.partial(_inproj_kernel, bt=bt, tm=tm, tn=tn, cm=cm, cn=min(cn, tn), dn=dn),
        grid=(b // bt, t // tm, n // tn),
        in_specs=in_specs,
        out_specs=out_specs,
        out_shape=out_shape,
        scratch_shapes=scratch,
        compiler_params=_cparams(("parallel", "arbitrary", "arbitrary")),
        name="inproj_dn" if dn else "inproj",
    )(*args)
    return res if dn else res[0]


ATTN_TQ = 256
ATTN_SUB = 128
ATTN_WIN = ATTN_SUB + A_BAND * CHUNK
ATTN_HEADS = 8


def _attn_kernel(q_ref, k0_ref, k1_ref, k2_ref, v0_ref, v1_ref, v2_ref, z_ref, bias_ref, o_ref, bias_scr):
    i = pl.program_id(2)
    nsub = ATTN_TQ // ATTN_SUB

    @pl.when(i <= 2)
    def _():
        for sb in range(nsub):
            col = lax.broadcasted_iota(jnp.int32, (1, ATTN_WIN), 1) + sb * ATTN_SUB
            missing = col < (2 - i) * ATTN_TQ
            for h in range(ATTN_HEADS):
                bias_scr[sb * ATTN_HEADS + h] = jnp.where(missing, NEG_BIG, bias_ref[h])

    q = q_ref[0] * (A_HEAD_DIM ** -0.5)
    k = jnp.concatenate([k0_ref[0], k1_ref[0], k2_ref[0]], axis=0)
    v = jnp.concatenate([v0_ref[0], v1_ref[0], v2_ref[0]], axis=0)
    items = [(h, sb) for h in range(ATTN_HEADS) for sb in range(nsub)]
    low = lax.broadcasted_iota(jnp.int32, (1, LANES), 1) < A_HEAD_DIM

    def pair(h):
        return slice(h // 2 * LANES, (h // 2 + 1) * LANES)

    def own(h):
        return low if h % 2 == 0 else jnp.logical_not(low)

    q_own = [jnp.where(own(h), q[:, pair(h)], jnp.zeros((), BF16)) for h in range(ATTN_HEADS)]
    v_ext = [jnp.where(own(h), v[:, pair(h)], jnp.ones((), BF16)) for h in range(ATTN_HEADS)]

    def scores(it):
        h, sb = it
        qs = q_own[h][sb * ATTN_SUB:(sb + 1) * ATTN_SUB]
        ks = k[sb * ATTN_SUB:sb * ATTN_SUB + ATTN_WIN, pair(h)]
        s = lax.dot_general(qs, ks, (((1,), (1,)), ((), ())), preferred_element_type=F32)
        return s + bias_scr[sb * ATTN_HEADS + h]

    def weights(s):
        return jnp.exp(s - jnp.max(s, axis=-1, keepdims=True)).astype(BF16)

    def attend(it, p):
        h, sb = it
        return jnp.dot(p, v_ext[h][sb * ATTN_SUB:sb * ATTN_SUB + ATTN_WIN], preferred_element_type=F32)

    ss = [scores(it) for it in items]
    ps = [weights(s) for s in ss]
    ol = {it: attend(it, p) for it, p in zip(items, ps)}

    def normalised(h0, sb):
        e0, e1 = ol[(h0, sb)], ol[(h0 + 1, sb)]
        return jnp.where(low, e0 / pltpu.roll(e0, A_HEAD_DIM, 1), e1 / pltpu.roll(e1, A_HEAD_DIM, 1))

    o = jnp.concatenate(
        [jnp.concatenate([normalised(h0, sb) for sb in range(nsub)], axis=0) for h0 in range(0, ATTN_HEADS, 2)],
        axis=1)
    o_ref[0] = (o * _silu(z_ref[0].astype(F32))).astype(o_ref.dtype)


def _band_bias_tile(rel_table):
    h = rel_table.shape[0]
    far = A_BAND * CHUNK
    n_diag = ATTN_WIN + ATTN_SUB - 1
    n_const = far + ATTN_SUB - 1 - REL_MAX + 1
    w = jnp.concatenate([jnp.broadcast_to(rel_table[:, 2 * REL_MAX:], (h, n_const)),
                         rel_table[:, 2 * REL_MAX - 1::-1][:, :n_diag - n_const]], axis=1).astype(F32)
    tiled =jnp.tile(jnp.pad(w, ((0, 0), (0, 1))), (1, ATTN_SUB))[:, :ATTN_SUB * n_diag]
    toep = tiled.reshape(h, ATTN_SUB, n_diag)[:, :, ATTN_SUB - 1:ATTN_SUB - 1 + ATTN_WIN]
    row = jnp.arange(ATTN_SUB)[:, None]
    colx = jnp.arange(ATTN_WIN)[None, :]
    rel = colx - (row // CHUNK) * CHUNK
    band = (rel >= 0) & (rel < (A_BAND + 1) * CHUNK)
    return jnp.where(band[None], toep, NEG_BIG)


def _attn_prompt(p0, bias):
    b, t, _ = p0.shape
    tq = ATTN_TQ
    width = ATTN_HEADS * A_HEAD_DIM
    per = A_WIDTH // width
    qo, ko, vo, zo = 0, per, 2 * per, 3 * per

    def blk(off, back):
        return pl.BlockSpec((1, tq, width), lambda hg, bb, i: (bb, jnp.maximum(i - back, 0), off + hg))

    return pl.pallas_call(
        _attn_kernel,
        grid=(per, b, t // tq),
        in_specs=[blk(qo, 0), blk(ko, 2), blk(ko, 1), blk(ko, 0), blk(vo, 2), blk(vo, 1), blk(vo, 0), blk(zo, 0),
                  pl.BlockSpec((ATTN_HEADS, ATTN_SUB, ATTN_WIN), lambda hg, bb, i: (hg, 0, 0))],
        out_specs=pl.BlockSpec((1, tq, width), lambda hg, bb, i: (bb, i, hg)),
        out_shape=jax.ShapeDtypeStruct((b, t, A_WIDTH), BF16),
        scratch_shapes=[pltpu.VMEM((ATTN_TQ // ATTN_SUB * ATTN_HEADS, ATTN_SUB, ATTN_WIN), F32)],
        compiler_params=_cparams(("parallel", "parallel", "arbitrary")),
        name="attn_prompt",
    )(p0, p0, p0, p0, p0, p0, p0, p0, bias)


def _attn_sample_kernel(q_ref, k_ref, v_ref, z_ref, ck_ref, cv_ref, bias_ref, o_ref, *, pad_rows):
    q = q_ref[0]
    zpad = jnp.zeros((pad_rows, A_WIDTH), BF16)
    k = jnp.concatenate([ck_ref[0].astype(BF16), k_ref[0], zpad], axis=0)
    v = jnp.concatenate([cv_ref[0].astype(BF16), v_ref[0], zpad], axis=0)
    def head(h):
        return slice(h * A_HEAD_DIM, (h + 1) * A_HEAD_DIM)

    ss = [lax.dot_general(q[:, head(h)], k[:, head(h)], (((1,), (1,)), ((), ())), preferred_element_type=F32)
          * (A_HEAD_DIM ** -0.5) + bias_ref[h] for h in range(A_HEADS)]
    ps = [jnp.exp(s - jnp.max(s, axis=-1, keepdims=True)) for s in ss]
    outs = [jnp.dot(p.astype(BF16), v[:, head(h)], preferred_element_type=F32) / jnp.sum(p, axis=-1, keepdims=True)
            for h, p in enumerate(ps)]
    o = jnp.concatenate(outs, axis=1)
    o_ref[0] = (o * _silu(z_ref[0].astype(F32))).astype(o_ref.dtype)


def _attn_sample(p0, cache_k, cache_v, rel_table):
    b, t, _ = p0.shape
    n_cache = cache_k.shape[1]
    n_keys = n_cache + t
    n_pad = -n_keys % LANES
    dist = jnp.arange(t)[:, None] + n_cache - jnp.arange(n_keys)[None, :]
    bias = rel_table[:, jnp.clip(dist, -REL_MAX, REL_MAX) + REL_MAX].astype(F32)
    bias = jnp.pad(bias, ((0, 0), (0, 0), (0, n_pad)), constant_values=NEG_BIG)
    ck = cache_k.reshape(b, n_cache, A_WIDTH)
    cv = cache_v.reshape(b, n_cache, A_WIDTH)

    def col(j):
        return pl.BlockSpec((1, t, A_WIDTH), lambda bb: (bb, 0, j))

    cache_spec = pl.BlockSpec((1, n_cache, A_WIDTH), lambda bb: (bb, 0, 0))
    return pl.pallas_call(
        functools.partial(_attn_sample_kernel, pad_rows=n_pad),
        grid=(b,),
        in_specs=[col(0), col(1), col(2), col(3), cache_spec, cache_spec,
                  pl.BlockSpec((A_HEADS, t, n_keys + n_pad), lambda bb: (0, 0, 0))],
        out_specs=pl.BlockSpec((1, t, A_WIDTH), lambda bb: (bb, 0, 0)),
        out_shape=jax.ShapeDtypeStruct((b, t, A_WIDTH), BF16),
        compiler_params=_cparams(("parallel",)),
        name="attn_sample",
    )(p0, p0, p0, p0, ck, cv, bias)


def _gmlp_kernel(u_ref, v_ref, z_ref, ws_ref, bst_ref, g_ref, b_ref, o_ref, *vn_out, n_mix, tb):
    vn = _layer_norm(_gelu(v_ref[0].astype(F32)), g_ref[...], b_ref[...])
    if vn_out:
        vn_out[0][0] = vn
    z = z_ref[0].astype(F32)
    gate = _gelu(u_ref[0].astype(F32)) * _silu(z)
    row = lax.broadcasted_iota(jnp.int32, (B_CHUNK, B_CHUNK), 0)
    colm = lax.broadcasted_iota(jnp.int32, (B_CHUNK, B_CHUNK), 1)
    keep = (colm <= row) & (row < n_mix)
    rows = min(tb, B_CHUNK)
    for g in range(B_GROUPS):
        gs = slice(g * B_GROUP_DIM, (g + 1) * B_GROUP_DIM)
        w = jnp.where(keep, ws_ref[g], 0.0).astype(BF16)
        bias = bst_ref[:, g:g + 1]
        for c in range(max(tb // B_CHUNK, 1)):
            rs = slice(c * B_CHUNK, c * B_CHUNK + rows)
            vg = vn[rs, gs].astype(BF16)
            if rows < B_CHUNK:
                vg = jnp.concatenate([vg, jnp.zeros((B_CHUNK - rows, B_GROUP_DIM), BF16)], axis=0)
            mix = jnp.dot(w, vg, preferred_element_type=F32) + bias
            o_ref[0, rs, gs] = (gate[rs, gs] * mix[:rows]).astype(o_ref.dtype)


def _gmlp(p0, w_s, b_s, ln_v_g, ln_v_b, *, tb, want_vn):
    b, t, _ = p0.shape
    tb = min(tb, t)
    n_mix = min(t, B_CHUNK)
    base = 4 * A_WIDTH // B_WIDTH

    def col(j):
        return pl.BlockSpec((1, tb, B_WIDTH), lambda bb, i: (bb, i, base + j))

    out_specs = [pl.BlockSpec((1, tb, B_WIDTH), lambda bb, i: (bb, i, 0))]
    out_shape = [jax.ShapeDtypeStruct((b, t, B_WIDTH), BF16)]
    if want_vn:
        out_specs.append(pl.BlockSpec((1, tb, B_WIDTH), lambda bb, i: (bb, i, 0)))
        out_shape.append(jax.ShapeDtypeStruct((b, t, B_WIDTH), F32))
    res = pl.pallas_call(
        functools.partial(_gmlp_kernel, n_mix=n_mix, tb=tb),
        grid=(b, t // tb),
        in_specs=[col(0), col(1), col(2),
                  pl.BlockSpec((B_GROUPS, B_CHUNK, B_CHUNK), lambda bb, i: (0, 0, 0)),
                  pl.BlockSpec((B_CHUNK, B_GROUPS), lambda bb, i: (0, 0)),
                  pl.BlockSpec((1, B_WIDTH), lambda bb, i: (0, 0)),
                  pl.BlockSpec((1, B_WIDTH), lambda bb, i: (0, 0))],
        out_specs=out_specs,
        out_shape=out_shape,
        compiler_params=_cparams(("parallel", "parallel")),
        name="gmlp",
    )(p0, p0, p0, w_s, b_s.T, ln_v_g.reshape(1, -1), ln_v_b.reshape(1, -1))
    return res if want_vn else res[0]


def _outproj_kernel(*refs, widths):
    o_refs = refs[:len(widths)]
    w_ref, x_ref, mod_ref, g_ref, b_ref, out_ref = refs[len(widths):]
    o = o_refs[0][0] if len(widths) == 1 else jnp.concatenate([o_ref[0] for o_ref in o_refs], axis=1)
    y = jnp.dot(o, w_ref[...], preferred_element_type=F32)
    r = DN_ALPHA * x_ref[0] + (1.0 + mod_ref[0, 2:3, :]) * y
    out_ref[0] = _layer_norm(r, g_ref[...], b_ref[...])


def _outproj(os_, w, x, mod, ln_g, ln_b, *, tm):
    b, t, d = x.shape
    tm = min(tm, t)
    widths = tuple(o.shape[-1] for o in os_)
    ktot = sum(widths)
    in_specs = [pl.BlockSpec((1, tm, kw), lambda bb, i: (bb, i, 0)) for kw in widths]
    in_specs += [
        pl.BlockSpec((ktot, d), lambda bb, i: (0, 0), pipeline_mode=pl.Buffered(1)),
        pl.BlockSpec((1, tm, d), lambda bb, i: (bb, i, 0)),
        pl.BlockSpec((1, 3, d), lambda bb, i: (bb, 0, 0)),
        pl.BlockSpec((1, d), lambda bb, i: (0, 0)),
        pl.BlockSpec((1, d), lambda bb, i: (0, 0)),
    ]
    return pl.pallas_call(
        functools.partial(_outproj_kernel, widths=widths),
        grid=(b, t // tm),
        in_specs=in_specs,
        out_specs=pl.BlockSpec((1, tm, d), lambda bb, i: (bb, i, 0)),
        out_shape=jax.ShapeDtypeStruct((b, t, d), F32),
        compiler_params=_cparams(("parallel", "parallel")),
        name="outproj",
    )(*os_, w, x, mod, ln_g.reshape(1, d), ln_b.reshape(1, d))


def _delta_kernel(q_ref, k_ref, v_ref, z_ref, bg_ref, gn_ref, s0_ref, o_ref, s_ref, gt_scr,
                  *, heads, nblk, blk, group):
    hg = pl.program_id(1)
    n = pl.program_id(2)

    @pl.when(n == 0)
    def _():
        s_ref[...] = s0_ref[...]

    def head_cols(j):
        return slice(j * C_HEAD_DIM, (j + 1) * C_HEAD_DIM)

    lane = lax.broadcasted_iota(jnp.int32, (blk, LANES), 1)
    ri = lax.broadcasted_iota(jnp.int32, (blk, blk), 0)
    ci = lax.broadcasted_iota(jnp.int32, (blk, blk), 1)
    incl = ci <= ri
    strict = ci < ri
    row2 = lax.broadcasted_iota(jnp.int32, (blk, 2 * blk), 0)
    lane2 = lax.broadcasted_iota(jnp.int32, (blk, 2 * blk), 1)
    hi_lanes = lane2 >= blk
    eye_hi = (lane2 == row2 + blk).astype(F32)
    gn = gn_ref[...]

    rows = [slice(nb * blk, (nb + 1) * blk) for nb in range(nblk)]
    hh_all = range(heads)
    beta_c, g_c, g_last, eg, kb, qk, amat, wq, u = {}, {}, {}, {}, {}, {}, {}, {}, {}

    def prepare(nbs):
        dec = {}
        for nb in nbs:
            bg = bg_ref[0, rows[nb], :]
            gt_scr[nb] = bg.T
            for hh in hh_all:
                it = (nb, hh)
                hidx = hg * heads + hh
                beta_c[it] = jnp.sum(jnp.where(lane == hidx, bg, 0.0), axis=1, keepdims=True)
                g_c[it] = jnp.sum(jnp.where(lane == C_V_HEADS + hidx, bg, 0.0), axis=1, keepdims=True)
                g_last[it] = g_c[it][blk - 1:blk, :]
                eg[it] = jnp.exp(g_c[it])
        yield
        for nb in nbs:
            for hh in hh_all:
                g_r = gt_scr[nb, pl.ds(C_V_HEADS + hg * heads + hh, 1), :]
                dec[(nb, hh)] = jnp.where(incl, jnp.exp(g_c[(nb, hh)] - g_r), 0.0)
        yield
        for nb in nbs:
            for j in range(heads // 2):
                h0, h1 = (nb, 2 * j), (nb, 2 * j + 1)
                kbf = k_ref[0, rows[nb], head_cols(j)]
                k = kbf.astype(F32)
                kb[h0] = k * beta_c[h0]
                kb[h1] = k * beta_c[h1]
                lhs = jnp.concatenate(
                    [q_ref[0, rows[nb], head_cols(j)], kb[h0].astype(BF16), kb[h1].astype(BF16)], axis=0)
                qa = lax.dot_general(lhs, kbf, (((1,), (1,)), ((), ())), preferred_element_type=F32)
                for m, it in enumerate((h0, h1)):
                    qk[it] = (qa[:blk] * dec[it]).astype(BF16)
                    amat[it] = jnp.where(strict, qa[(m + 1) * blk:(m + 2) * blk] * dec[it], 0.0)
        yield

    def invert(nbs):
        its = [(nb, hh) for nb in nbs for hh in hh_all]
        c = {}
        for it in its:
            a = amat.pop(it)
            c[it] = eye_hi - jnp.concatenate([a, jnp.zeros_like(a)], axis=1)
        span = 1
        while span < blk:
            for it in its:
                cb = c[it].astype(BF16)
                c[it] = jnp.dot(cb[:, :blk], cb, preferred_element_type=F32) + jnp.where(hi_lanes, c[it], 0.0)
            yield
            span *= 2
        for it in its:
            t = c[it][:, blk:]
            nb, hh = it
            v = v_ref[0, rows[nb], head_cols(hh)].astype(F32)
            rhs = jnp.concatenate([kb[it] * eg[it], v * beta_c[it]], axis=1).astype(BF16)
            wu = jnp.dot(t.astype(BF16), rhs, preferred_element_type=F32)
            qd = q_ref[0, rows[nb], head_cols(hh // 2)].astype(F32) * eg[it]
            wq[it] = jnp.concatenate([wu[:, :C_HEAD_DIM], qd], axis=0).astype(BF16)
            u[it] = wu[:, C_HEAD_DIM:]
        yield

    s = [s_ref[0, hh] for hh in hh_all]

    def scan(nbs):
        for nb in nbs:
            ws_qs = [jnp.dot(wq.pop((nb, hh)), s[hh].astype(BF16), preferred_element_type=F32) for hh in hh_all]
            yield
            vnb = [(u.pop((nb, hh)) - ws_qs[hh][:blk]).astype(BF16) for hh in hh_all]
            for hh in hh_all:
                it = (nb, hh)
                k = k_ref[0, rows[nb], head_cols(hh // 2)].astype(F32)
                k_dec = (k * jnp.exp(g_last[it] - g_c[it])).astype(BF16)
                s[hh] = s[hh] * jnp.exp(g_last[it]) + lax.dot_general(
                    k_dec, vnb[hh], (((0,), (0,)), ((), ())), preferred_element_type=F32)
            yield
            for hh in hh_all:
                o = ws_qs[hh][blk:] + jnp.dot(qk.pop((nb, hh)), vnb[hh], preferred_element_type=F32)
                on = o * lax.rsqrt(jnp.mean(o * o, axis=-1, keepdims=True) + NORM_EPS) * gn
                hs = head_cols(hh)
                o_ref[0, rows[nb], hs] = (on * _silu(z_ref[0, rows[nb], hs].astype(F32))).astype(o_ref.dtype)
            yield

    def round_robin(gens):
        gens = list(gens)
        while gens:
            for g in list(gens):
                try:
                    next(g)
                except StopIteration:
                    gens.remove(g)

    gsz = min(group, nblk)
    groups = [list(range(g0, g0 + gsz)) for g0 in range(0, nblk, gsz)]
    round_robin([prepare(groups[0])])
    for gi in range(len(groups) + 1):
        active = []
        if gi < len(groups):
            active.append(invert(groups[gi]))
        if gi + 1 < len(groups):
            active.append(prepare(groups[gi + 1]))
        if gi >= 1:
            active.append(scan(groups[gi - 1]))
        round_robin(active)
    for hh in hh_all:
        s_ref[0, hh] = s[hh]


def _delta(p1, bg, s0, o_norm_g, *, heads, nblk, group=4):
    b, t, _ = p1.shape
    blk = CHUNK
    tb = nblk * blk
    cq = heads // 2 * C_HEAD_DIM
    cv = heads * C_HEAD_DIM
    ko, vo, zo = C_QK_WIDTH // cq, 2 * C_QK_WIDTH // cv, C_CONV_CH // cv

    def tok(width, off):
        return pl.BlockSpec((1, tb, width), lambda bb, hg, n: (bb, n, off + hg))

    state_spec = pl.BlockSpec((1, heads, C_HEAD_DIM, C_HEAD_DIM), lambda bb, hg, n: (bb, hg, 0, 0))
    return pl.pallas_call(
        functools.partial(_delta_kernel, heads=heads, nblk=nblk, blk=blk, group=group),
        grid=(b, C_V_HEADS // heads, t // tb),
        in_specs=[tok(cq, 0), tok(cq, ko), tok(cv, vo), tok(cv, zo),
                  pl.BlockSpec((1, tb, LANES), lambda bb, hg, n: (bb, n, 0)),
                  pl.BlockSpec((1, C_HEAD_DIM), lambda bb, hg, n: (0, 0)),
                  state_spec],
        out_specs=[pl.BlockSpec((1, tb, cv), lambda bb, hg, n: (bb, n, hg)), state_spec],
        out_shape=[jax.ShapeDtypeStruct((b, t, C_V_WIDTH), BF16),
                   jax.ShapeDtypeStruct((b, C_V_HEADS, C_HEAD_DIM, C_HEAD_DIM), F32)],
        scratch_shapes=[pltpu.VMEM((nblk, LANES, blk), F32)],
        compiler_params=_cparams(("parallel", "parallel", "arbitrary")),
        name="delta",
    )(p1, p1, p1, p1, bg, o_norm_g.reshape(1, C_HEAD_DIM), s0)


def _trunk(x, mods, wts, cache_k, cache_v, conv_left, s0, *, prompt):
    (w_in_ab, bias_blocks, rel_table, ln_v_g, ln_v_b, w_s, b_s, w_out_ab, w_in_dn, w_extra, w_conv, ad_rows,
     o_norm_g, w_out_dn, ln_g, ln_b) = wts
    b, t, _ = x.shape
    p0 = _inproj(x, mods[0], w_in_ab, tm=1024, tn=1024, cm=256)
    if prompt:
        o_a = _attn_prompt(p0, bias_blocks)
        o_b = _gmlp(p0, w_s, b_s, ln_v_g, ln_v_b, tb=512, want_vn=False)
        v_n = None
        keep = min(A_BAND * CHUNK, t)
    else:
        o_a = _attn_sample(p0, cache_k, cache_v, rel_table)
        o_b, v_n = _gmlp(p0, w_s, b_s, ln_v_g, ln_v_b, tb=256, want_vn=True)
        keep = t
    new_k = p0[:, t - keep:, A_WIDTH:2 * A_WIDTH].astype(F32).reshape(b, keep, A_HEADS, A_HEAD_DIM)
    new_v = p0[:, t - keep:, 2 * A_WIDTH:3 * A_WIDTH].astype(F32).reshape(b, keep, A_HEADS, A_HEAD_DIM)
    x1 = _outproj([o_a, o_b], w_out_ab, x, mods[0], ln_g[0], ln_b[0], tm=512)
    if conv_left is None:
        left8 = jnp.zeros((b, HALO_ROWS, C_CONV_CH), F32)
    else:
        left8 = jnp.pad(conv_left, ((0, 0), (HALO_ROWS - (C_CONV - 1), 0), (0, 0)))
    p1, bg, tail = _inproj(x1, mods[1], w_in_dn, (w_extra, ad_rows, w_conv, left8),
                           tm=1024, tn=1024, cm=256, cn=1024, n_out=C_MAIN)
    new_conv = tail[:, -1, HALO_ROWS - (C_CONV - 1):, :]
    if s0 is None:
        s0 = jnp.zeros((b, C_V_HEADS, C_HEAD_DIM, C_HEAD_DIM), F32)
    t_pad = -t % CHUNK
    if t_pad:
        p1 = jnp.pad(p1, ((0, 0), (0, t_pad), (0, 0)))
        bg = jnp.concatenate([jnp.pad(bg[..., :C_V_HEADS], ((0, 0), (0, t_pad), (0, 0))),
                              jnp.pad(bg[..., C_V_HEADS:], ((0, 0), (0, t_pad), (0, 0)), mode="edge")], axis=-1)
    nblk = max(n for n in (4, 2, 1) if (t + t_pad) % (n * CHUNK) == 0)
    o_c, s_new = _delta(p1, bg, s0, o_norm_g, heads=min(C_V_HEADS, 32 // nblk), nblk=nblk)
    if t_pad:
        o_c = o_c[:, :t]
    x2 = _outproj([o_c], w_out_dn, x1, mods[1], ln_g[1], ln_b[1], tm=512)
    return x2, new_k, new_v, v_n, new_conv, s_new


def kernel(x_prompt, x_sample, cache_a_k, cache_a_v, state_c_conv, state_c_s, c_prompt, c_sample, w_ada, b_ada,
           ln_g, ln_b, w_in_ab, rel_table, ln_v_g, ln_v_b, w_s, b_s, w_out_ab, w_in_dn, w_conv, a_log, dt_bias,
           o_norm_g, w_out_dn):
    bp = c_prompt.shape[0]
    bs = c_sample.shape[0]
    c_all = jnp.concatenate([c_prompt, c_sample], axis=0)
    c_all = jnp.pad(c_all, ((0, -(bp + bs) % 16), (0, 0)))
    ada = _ada(c_all, w_ada, b_ada)
    mods_p = [ada[l, :bp].reshape(bp, 3, D_MODEL) for l in range(DEPTH)]
    mods_s = [ada[l, bp:bp + bs].reshape(bs, 3, D_MODEL) for l in range(DEPTH)]

    extra = w_in_dn[:, C_MAIN:]
    w_extra = jnp.pad(extra, ((0, 0), (0, LANES - extra.shape[1]))).astype(BF16)
    ad_rows = jnp.zeros((2, LANES), F32)
    ad_rows = ad_rows.at[0, C_V_HEADS:2 * C_V_HEADS].set(a_log).at[1, C_V_HEADS:2 * C_V_HEADS].set(dt_bias)
    wts = (w_in_ab.astype(BF16), _band_bias_tile(rel_table), rel_table, ln_v_g, ln_v_b, w_s, b_s,
           w_out_ab.astype(BF16), w_in_dn.astype(BF16), w_extra, w_conv, ad_rows, o_norm_g,
           w_out_dn.astype(BF16), ln_g, ln_b)

    y_p, p_a_k, p_a_v, _, p_c_conv, p_c_s = _trunk(x_prompt, mods_p, wts, None, None, None, None, prompt=True)
    y_s, s_a_k, s_a_v, s_b_v, s_c_conv, s_c_s = _trunk(x_sample, mods_s, wts, cache_a_k, cache_a_v, state_c_conv,
                                                       state_c_s, prompt=False)
    s_b_v = s_b_v.reshape(bs, -1, B_GROUPS, B_GROUP_DIM)
    return (y_p, y_s, p_a_k, p_a_v, p_c_conv, p_c_s, s_a_k, s_a_v, s_b_v, s_c_conv, s_c_s)
```

```python
import functools

import jax
import jax.numpy as jnp
from jax import lax
from jax.experimental import pallas as pl
from jax.experimental.pallas import tpu as pltpu

F32 = jnp.float32
BF16 = jnp.bfloat16

D_MODEL = 2048
DEPTH = 2
CHUNK = 64
A_HEADS = 16
A_HEAD_DIM = 64
A_WIDTH = A_HEADS * A_HEAD_DIM
A_BAND = 8
REL_MAX = 128
B_GROUPS = 8
B_GROUP_DIM = 128
B_WIDTH = B_GROUPS * B_GROUP_DIM
B_CHUNK = 128
C_QK_HEADS = 16
C_V_HEADS = 32
C_HEAD_DIM = 128
C_QK_WIDTH = C_QK_HEADS * C_HEAD_DIM
C_V_WIDTH = C_V_HEADS * C_HEAD_DIM
C_CONV = 4
C_CONV_CH = 2 * C_QK_WIDTH + C_V_WIDTH
C_MAIN = C_CONV_CH + C_V_WIDTH
DN_ALPHA = (2 * DEPTH) ** 0.25
LN_EPS = 1e-5
NORM_EPS = 1e-6
NEG_BIG = -1e30

LANES = 128
HALO_ROWS = 8
VMEM_LIMIT = 56 * 1024 * 1024

PROJ_TM = 1024
PROJ_TN = 1024
PROJ_CM = 256
OUT_TM = 512
GMLP_TB = 512
DELTA_CHAINS = 64


def _cparams(sem):
    return pltpu.CompilerParams(dimension_semantics=sem, vmem_limit_bytes=VMEM_LIMIT)


def _silu(x):
    return x * jax.nn.sigmoid(x)


def _gelu(x):
    return 0.5 * x * (1.0 + lax.erf(x * (2.0 ** -0.5)))


def _layer_norm(x, g, b):
    mu = jnp.mean(x, axis=-1, keepdims=True)
    xc = x - mu
    var = jnp.mean(xc * xc, axis=-1, keepdims=True)
    return xc * lax.rsqrt(var + LN_EPS) * g + b


def _ada_kernel(c_ref, w_ref, b_ref, o_ref):
    c = c_ref[...].astype(BF16)
    w = w_ref[0].astype(BF16)
    o_ref[0] = jnp.dot(c, w, preferred_element_type=F32) + b_ref[0]


def _ada(c_all, w_ada, b_ada):
    rows, d = c_all.shape
    depth, _, n = w_ada.shape
    tn = 768
    return pl.pallas_call(
        _ada_kernel,
        grid=(depth, n // tn),
        in_specs=[
            pl.BlockSpec((rows, d), lambda l, j: (0, 0)),
            pl.BlockSpec((1, d, tn), lambda l, j: (l, 0, j)),
            pl.BlockSpec((1, 1, tn), lambda l, j: (l, 0, j)),
        ],
        out_specs=pl.BlockSpec((1, rows, tn), lambda l, j: (l, 0, j)),
        out_shape=jax.ShapeDtypeStruct((depth, rows, n), F32),
        compiler_params=_cparams(("parallel", "parallel")),
        name="ada",
    )(c_all, w_ada, b_ada.reshape(depth, 1, n))


def _inproj_kernel(*refs, bt, tm, tn, cm, cn, dn):
    if dn:
        x_ref, mod_ref, w_ref, we_ref, ad_ref, wc_ref, cl_ref, o_ref, oe_ref, tail_ref, h_scr, halo_scr = refs
    else:
        x_ref, mod_ref, w_ref, o_ref, h_scr = refs
    i = pl.program_id(1)
    j = pl.program_id(2)
    rows = bt * tm

    @pl.when(j == 0)
    def _():
        h = x_ref[...] * (1.0 + mod_ref[:, 1:2, :]) + mod_ref[:, 0:1, :]
        hb = h.reshape(rows, h.shape[-1]).astype(BF16)
        h_scr[...] = hb
        if dn:
            raw = jnp.dot(hb, we_ref[...], preferred_element_type=F32)
            beta = jax.nn.sigmoid(raw)
            xa = raw + ad_ref[1:2, :]
            softplus = jnp.maximum(xa, 0.0) + jnp.log1p(jnp.exp(-jnp.abs(xa)))
            la = -jnp.exp(ad_ref[0:1, :]) * softplus
            gb = min(CHUNK, tm)
            grp = min(rows, 4 * CHUNK)
            r = lax.broadcasted_iota(jnp.int32, (grp, grp), 0)
            c = lax.broadcasted_iota(jnp.int32, (grp, grp), 1)
            sh = gb.bit_length() - 1
            same = lax.shift_right_logical(r, sh) == lax.shift_right_logical(c, sh)
            csum = ((c <= r) & same).astype(BF16)
            hi = la.astype(BF16)
            rem = la - hi.astype(F32)
            mid = rem.astype(BF16)
            lo = (rem - mid.astype(F32)).astype(BF16)
            g = jnp.concatenate(
                [sum(jnp.dot(csum, part[g0:g0 + grp], preferred_element_type=F32) for part in (hi, mid, lo))
                 for g0 in range(0, rows, grp)], axis=0)
            lane = lax.broadcasted_iota(jnp.int32, raw.shape, 1)
            bg = jnp.where(lane < C_V_HEADS, beta, g)
            oe_ref[...] = bg.reshape(bt, tm, bg.shape[-1])

    def plain():
        acc = jnp.dot(h_scr[...], w_ref[...], preferred_element_type=F32)
        o_ref[...] = acc.reshape(bt, tm, tn).astype(o_ref.dtype)

    if not dn:
        plain()
        return

    n_conv = C_CONV_CH // tn

    def conv_tiles(l2norm):
        qscale = jnp.where(j < C_QK_WIDTH // tn, C_HEAD_DIM ** -0.5, 1.0)
        row8 = lax.broadcasted_iota(jnp.int32, (HALO_ROWS, cn), 0)
        ncp = tn // cn

        def cols(nt):
            return slice(nt * cn, (nt + 1) * cn)

        def epilogue(seg, prev, nt):
            taps = wc_ref[:, cols(nt)]
            y = taps[C_CONV - 1:C_CONV] * seg
            for delay in range(1, C_CONV):
                rolled = pltpu.roll(seg, delay, 0)
                first = jnp.where(row8 < delay, pltpu.roll(prev, delay, 0), rolled[:HALO_ROWS])
                shifted = jnp.concatenate([first, rolled[HALO_ROWS:]], axis=0)
                y = y + taps[C_CONV - 1 - delay:C_CONV - delay] * shifted
            s = _silu(y)
            if not l2norm:
                return s.astype(o_ref.dtype)
            outs = []
            for hd in range(cn // C_HEAD_DIM):
                sg = s[:, hd * C_HEAD_DIM:(hd + 1) * C_HEAD_DIM]
                outs.append(sg * (lax.rsqrt(jnp.sum(sg * sg, axis=-1, keepdims=True) + NORM_EPS) * qscale))
            return jnp.concatenate(outs, axis=1).astype(o_ref.dtype)

        if bt == 1:
            @pl.when(i == 0)
            def _():
                halo_scr[j] = cl_ref[0]

            prev = {nt: halo_scr[j, :, cols(nt)] for nt in range(ncp)}
            pieces = [(c, nt) for c in range(tm // cm) for nt in range(ncp)]
            pending = None
            for p in range(len(pieces) + 1):
                issued = None
                if p < len(pieces):
                    c, nt = pieces[p]
                    issued = (c, nt, jnp.dot(h_scr[c * cm:(c + 1) * cm, :], w_ref[:, cols(nt)],
                                             preferred_element_type=F32))
                if pending is not None:
                    c, nt, a = pending
                    o_ref[0, c * cm:(c + 1) * cm, cols(nt)] = epilogue(a, prev[nt], nt)
                    prev[nt] = a[cm - HALO_ROWS:cm]
                pending = issued
            for nt in range(ncp):
                halo_scr[j, :, cols(nt)] = prev[nt]
                tail_ref[0, 0, :, cols(nt)] = prev[nt]
        else:
            for nt in range(ncp):
                acc = jnp.dot(h_scr[...], w_ref[:, cols(nt)], preferred_element_type=F32)
                for bi in range(bt):
                    seg = acc[bi * tm:(bi + 1) * tm]
                    o_ref[bi, :, cols(nt)] = epilogue(seg, cl_ref[bi, :, cols(nt)], nt)
                    tail_ref[bi, 0, :, cols(nt)] = seg[tm - HALO_ROWS:tm]

    n_qk = (2 * C_QK_WIDTH) // tn
    pl.when(j < n_qk)(functools.partial(conv_tiles, True))
    pl.when((j >= n_qk) & (j < n_conv))(functools.partial(conv_tiles, False))
    pl.when(j >= n_conv)(plain)


def _inproj(x, mod, w, dn_args=None, *, tm, tn, cm, cn=256, n_out=None):
    b, t, d = x.shape
    n = n_out or w.shape[1]
    tm = min(tm, t)
    bt = b if t == tm and b * t <= 1024 else 1
    cm = min(cm, tm)
    dn = dn_args is not None
    in_specs = [
        pl.BlockSpec((bt, tm, d), lambda bb, i, j: (bb, i, 0)),
        pl.BlockSpec((bt, 3, d), lambda bb, i, j: (bb, 0, 0)),
        pl.BlockSpec((d, tn), lambda bb, i, j: (0, j)),
    ]
    out_specs = [pl.BlockSpec((bt, tm, tn), lambda bb, i, j: (bb, i, j))]
    out_shape = [jax.ShapeDtypeStruct((b, t, n), BF16)]
    scratch = [pltpu.VMEM((bt * tm, d), BF16)]
    args = [x, mod, w]
    if dn:
        w_extra, ad_rows, w_conv, conv_left8 = dn_args
        ne = w_extra.shape[1]
        last = C_CONV_CH // tn - 1

        def cj(j):
            return jnp.minimum(j, last)

        in_specs += [
            pl.BlockSpec((d, ne), lambda bb, i, j: (0, 0)),
            pl.BlockSpec((2, ne), lambda bb, i, j: (0, 0)),
            pl.BlockSpec((C_CONV, tn), lambda bb, i, j: (0, cj(j))),
            pl.BlockSpec((bt, HALO_ROWS, tn), lambda bb, i, j: (bb, 0, cj(j))),
        ]
        out_specs += [
            pl.BlockSpec((bt, tm, ne), lambda bb, i, j: (bb, i, 0)),
            pl.BlockSpec((bt, 1, HALO_ROWS, tn), lambda bb, i, j: (bb, i, 0, cj(j))),
        ]
        out_shape += [jax.ShapeDtypeStruct((b, t, ne), F32),
                      jax.ShapeDtypeStruct((b, t // tm, HALO_ROWS, C_CONV_CH), F32)]
        scratch.append(pltpu.VMEM((C_CONV_CH // tn, HALO_ROWS, tn), F32))
        args += [w_extra, ad_rows, w_conv, conv_left8]
    res = pl.pallas_call(
        functools.partial(_inproj_kernel, bt=bt, tm=tm, tn=tn, cm=cm, cn=min(cn, tn), dn=dn),
        grid=(b // bt, t // tm, n // tn),
        in_specs=in_specs,
        out_specs=out_specs,
        out_shape=out_shape,
        scratch_shapes=scratch,
        compiler_params=_cparams(("parallel", "arbitrary", "arbitrary")),
        name="inproj_dn" if dn else "inproj",
    )(*args)
    return res if dn else res[0]


ATTN_TQ = 256
ATTN_SUB = 128
ATTN_WIN = ATTN_SUB + A_BAND * CHUNK
ATTN_HEADS = 8


def _attn_kernel(q_ref, k0_ref, k1_ref, k2_ref, v0_ref, v1_ref, v2_ref, z_ref, bias_ref, o_ref, bias_scr):
    i = pl.program_id(2)
    nsub = ATTN_TQ // ATTN_SUB

    @pl.when(i <= 2)
    def _():
        for sb in range(nsub):
            col = lax.broadcasted_iota(jnp.int32, (1, ATTN_WIN), 1) + sb * ATTN_SUB
            missing = col < (2 - i) * ATTN_TQ
            for h in range(ATTN_HEADS):
                bias_scr[sb * ATTN_HEADS + h] = jnp.where(missing, NEG_BIG, bias_ref[h])

    q = q_ref[0] * (A_HEAD_DIM ** -0.5)
    k = jnp.concatenate([k0_ref[0], k1_ref[0], k2_ref[0]], axis=0)
    v = jnp.concatenate([v0_ref[0], v1_ref[0], v2_ref[0]], axis=0)
    items = [(h, sb) for h in range(ATTN_HEADS) for sb in range(nsub)]
    low = lax.broadcasted_iota(jnp.int32, (1, LANES), 1) < A_HEAD_DIM

    def pair(h):
        return slice(h // 2 * LANES, (h // 2 + 1) * LANES)

    def own(h):
        return low if h % 2 == 0 else jnp.logical_not(low)

    q_own = [jnp.where(own(h), q[:, pair(h)], jnp.zeros((), BF16)) for h in range(ATTN_HEADS)]
    v_ext = [jnp.where(own(h), v[:, pair(h)], jnp.ones((), BF16)) for h in range(ATTN_HEADS)]

    def scores(it):
        h, sb = it
        qs = q_own[h][sb * ATTN_SUB:(sb + 1) * ATTN_SUB]
        ks = k[sb * ATTN_SUB:sb * ATTN_SUB + ATTN_WIN, pair(h)]
        s = lax.dot_general(qs, ks, (((1,), (1,)), ((), ())), preferred_element_type=F32)
        return s + bias_scr[sb * ATTN_HEADS + h]

    def weights(s):
        return jnp.exp(s - jnp.max(s, axis=-1, keepdims=True)).astype(BF16)

    def attend(it, p):
        h, sb = it
        return jnp.dot(p, v_ext[h][sb * ATTN_SUB:sb * ATTN_SUB + ATTN_WIN], preferred_element_type=F32)

    ss = [scores(it) for it in items]
    ps = [weights(s) for s in ss]
    ol = {it: attend(it, p) for it, p in zip(items, ps)}

    def normalised(h0, sb):
        e0, e1 = ol[(h0, sb)], ol[(h0 + 1, sb)]
        return jnp.where(low, e0 / pltpu.roll(e0, A_HEAD_DIM, 1), e1 / pltpu.roll(e1, A_HEAD_DIM, 1))

    o = jnp.concatenate(
        [jnp.concatenate([normalised(h0, sb) for sb in range(nsub)], axis=0) for h0 in range(0, ATTN_HEADS, 2)],
        axis=1)
    o_ref[0] = (o * _silu(z_ref[0].astype(F32))).astype(o_ref.dtype)


def _band_bias_tile(rel_table):
    h = rel_table.shape[0]
    far = A_BAND * CHUNK
    n_diag = ATTN_WIN + ATTN_SUB - 1
    n_const = far + ATTN_SUB - 1 - REL_MAX + 1
    w = jnp.concatenate([jnp.broadcast_to(rel_table[:, 2 * REL_MAX:], (h, n_const)),
                         rel_table[:, 2 * REL_MAX - 1::-1][:, :n_diag - n_const]], axis=1).astype(F32)
    tiled =jnp.tile(jnp.pad(w, ((0, 0), (0, 1))), (1, ATTN_SUB))[:, :ATTN_SUB * n_diag]
    toep = tiled.reshape(h, ATTN_SUB, n_diag)[:, :, ATTN_SUB - 1:ATTN_SUB - 1 + ATTN_WIN]
    row = jnp.arange(ATTN_SUB)[:, None]
    colx = jnp.arange(ATTN_WIN)[None, :]
    rel = colx - (row // CHUNK) * CHUNK
    band = (rel >= 0) & (rel < (A_BAND + 1) * CHUNK)
    return jnp.where(band[None], toep, NEG_BIG)


def _attn_prompt(p0, bias):
    b, t, _ = p0.shape
    tq = ATTN_TQ
    width = ATTN_HEADS * A_HEAD_DIM
    per = A_WIDTH // width
    qo, ko, vo, zo = 0, per, 2 * per, 3 * per

    def blk(off, back):
        return pl.BlockSpec((1, tq, width), lambda hg, bb, i: (bb, jnp.maximum(i - back, 0), off + hg))

    return pl.pallas_call(
        _attn_kernel,
        grid=(per, b, t // tq),
        in_specs=[blk(qo, 0), blk(ko, 2), blk(ko, 1), blk(ko, 0), blk(vo, 2), blk(vo, 1), blk(vo, 0), blk(zo, 0),
                  pl.BlockSpec((ATTN_HEADS, ATTN_SUB, ATTN_WIN), lambda hg, bb, i: (hg, 0, 0))],
        out_specs=pl.BlockSpec((1, tq, width), lambda hg, bb, i: (bb, i, hg)),
        out_shape=jax.ShapeDtypeStruct((b, t, A_WIDTH), BF16),
        scratch_shapes=[pltpu.VMEM((ATTN_TQ // ATTN_SUB * ATTN_HEADS, ATTN_SUB, ATTN_WIN), F32)],
        compiler_params=_cparams(("parallel", "parallel", "arbitrary")),
        name="attn_prompt",
    )(p0, p0, p0, p0, p0, p0, p0, p0, bias)


def _attn_sample_kernel(q_ref, k_ref, v_ref, z_ref, ck_ref, cv_ref, bias_ref, o_ref, *, pad_rows):
    q = q_ref[0]
    zpad = jnp.zeros((pad_rows, A_WIDTH), BF16)
    k = jnp.concatenate([ck_ref[0].astype(BF16), k_ref[0], zpad], axis=0)
    v = jnp.concatenate([cv_ref[0].astype(BF16), v_ref[0], zpad], axis=0)
    def head(h):
        return slice(h * A_HEAD_DIM, (h + 1) * A_HEAD_DIM)

    ss = [lax.dot_general(q[:, head(h)], k[:, head(h)], (((1,), (1,)), ((), ())), preferred_element_type=F32)
          * (A_HEAD_DIM ** -0.5) + bias_ref[h] for h in range(A_HEADS)]
    ps = [jnp.exp(s - jnp.max(s, axis=-1, keepdims=True)) for s in ss]
    outs = [jnp.dot(p.astype(BF16), v[:, head(h)], preferred_element_type=F32) / jnp.sum(p, axis=-1, keepdims=True)
            for h, p in enumerate(ps)]
    o = jnp.concatenate(outs, axis=1)
    o_ref[0] = (o * _silu(z_ref[0].astype(F32))).astype(o_ref.dtype)


def _attn_sample(p0, cache_k, cache_v, rel_table):
    b, t, _ = p0.shape
    n_cache = cache_k.shape[1]
    n_keys = n_cache + t
    n_pad = -n_keys % LANES
    dist = jnp.arange(t)[:, None] + n_cache - jnp.arange(n_keys)[None, :]
    bias = rel_table[:, jnp.clip(dist, -REL_MAX, REL_MAX) + REL_MAX].astype(F32)
    bias = jnp.pad(bias, ((0, 0), (0, 0), (0, n_pad)), constant_values=NEG_BIG)
    ck = cache_k.reshape(b, n_cache, A_WIDTH)
    cv = cache_v.reshape(b, n_cache, A_WIDTH)

    def col(j):
        return pl.BlockSpec((1, t, A_WIDTH), lambda bb: (bb, 0, j))

    cache_spec = pl.BlockSpec((1, n_cache, A_WIDTH), lambda bb: (bb, 0, 0))
    return pl.pallas_call(
        functools.partial(_attn_sample_kernel, pad_rows=n_pad),
        grid=(b,),
        in_specs=[col(0), col(1), col(2), col(3), cache_spec, cache_spec,
                  pl.BlockSpec((A_HEADS, t, n_keys + n_pad), lambda bb: (0, 0, 0))],
        out_specs=pl.BlockSpec((1, t, A_WIDTH), lambda bb: (bb, 0, 0)),
        out_shape=jax.ShapeDtypeStruct((b, t, A_WIDTH), BF16),
        compiler_params=_cparams(("parallel",)),
        name="attn_sample",
    )(p0, p0, p0, p0, ck, cv, bias)


def _gmlp_kernel(u_ref, v_ref, z_ref, ws_ref, bst_ref, g_ref, b_ref, o_ref, *vn_out, n_mix, tb):
    vn = _layer_norm(_gelu(v_ref[0].astype(F32)), g_ref[...], b_ref[...])
    if vn_out:
        vn_out[0][0] = vn
    z = z_ref[0].astype(F32)
    gate = _gelu(u_ref[0].astype(F32)) * _silu(z)
    row = lax.broadcasted_iota(jnp.int32, (B_CHUNK, B_CHUNK), 0)
    colm = lax.broadcasted_iota(jnp.int32, (B_CHUNK, B_CHUNK), 1)
    keep = (colm <= row) & (row < n_mix)
    rows = min(tb, B_CHUNK)
    for g in range(B_GROUPS):
        gs = slice(g * B_GROUP_DIM, (g + 1) * B_GROUP_DIM)
        w = jnp.where(keep, ws_ref[g], 0.0).astype(BF16)
        bias = bst_ref[:, g:g + 1]
        for c in range(max(tb // B_CHUNK, 1)):
            rs = slice(c * B_CHUNK, c * B_CHUNK + rows)
            vg = vn[rs, gs].astype(BF16)
            if rows < B_CHUNK:
                vg = jnp.concatenate([vg, jnp.zeros((B_CHUNK - rows, B_GROUP_DIM), BF16)], axis=0)
            mix = jnp.dot(w, vg, preferred_element_type=F32) + bias
            o_ref[0, rs, gs] = (gate[rs, gs] * mix[:rows]).astype(o_ref.dtype)


def _gmlp(p0, w_s, b_s, ln_v_g, ln_v_b, *, tb, want_vn):
    b, t, _ = p0.shape
    tb = min(tb, t)
    n_mix = min(t, B_CHUNK)
    base = 4 * A_WIDTH // B_WIDTH

    def col(j):
        return pl.BlockSpec((1, tb, B_WIDTH), lambda bb, i: (bb, i, base + j))

    out_specs = [pl.BlockSpec((1, tb, B_WIDTH), lambda bb, i: (bb, i, 0))]
    out_shape = [jax.ShapeDtypeStruct((b, t, B_WIDTH), BF16)]
    if want_vn:
        out_specs.append(pl.BlockSpec((1, tb, B_WIDTH), lambda bb, i: (bb, i, 0)))
        out_shape.append(jax.ShapeDtypeStruct((b, t, B_WIDTH), F32))
    res = pl.pallas_call(
        functools.partial(_gmlp_kernel, n_mix=n_mix, tb=tb),
        grid=(b, t // tb),
        in_specs=[col(0), col(1), col(2),
                  pl.BlockSpec((B_GROUPS, B_CHUNK, B_CHUNK), lambda bb, i: (0, 0, 0)),
                  pl.BlockSpec((B_CHUNK, B_GROUPS), lambda bb, i: (0, 0)),
                  pl.BlockSpec((1, B_WIDTH), lambda bb, i: (0, 0)),
                  pl.BlockSpec((1, B_WIDTH), lambda bb, i: (0, 0))],
        out_specs=out_specs,
        out_shape=out_shape,
        compiler_params=_cparams(("parallel", "parallel")),
        name="gmlp",
    )(p0, p0, p0, w_s, b_s.T, ln_v_g.reshape(1, -1), ln_v_b.reshape(1, -1))
    return res if want_vn else res[0]


def _outproj_kernel(*refs, widths):
    o_refs = refs[:len(widths)]
    w_ref, x_ref, mod_ref, g_ref, b_ref, out_ref = refs[len(widths):]
    o = o_refs[0][0] if len(widths) == 1 else jnp.concatenate([o_ref[0] for o_ref in o_refs], axis=1)
    y = jnp.dot(o, w_ref[...], preferred_element_type=F32)
    r = DN_ALPHA * x_ref[0] + (1.0 + mod_ref[0, 2:3, :]) * y
    out_ref[0] = _layer_norm(r, g_ref[...], b_ref[...])


def _outproj(os_, w, x, mod, ln_g, ln_b, *, tm):
    b, t, d = x.shape
    tm = min(tm, t)
    widths = tuple(o.shape[-1] for o in os_)
    ktot = sum(widths)
    in_specs = [pl.BlockSpec((1, tm, kw), lambda bb, i: (bb, i, 0)) for kw in widths]
    in_specs += [
        pl.BlockSpec((ktot, d), lambda bb, i: (0, 0), pipeline_mode=pl.Buffered(1)),
        pl.BlockSpec((1, tm, d), lambda bb, i: (bb, i, 0)),
        pl.BlockSpec((1, 3, d), lambda bb, i: (bb, 0, 0)),
        pl.BlockSpec((1, d), lambda bb, i: (0, 0)),
        pl.BlockSpec((1, d), lambda bb, i: (0, 0)),
    ]
    return pl.pallas_call(
        functools.partial(_outproj_kernel, widths=widths),
        grid=(b, t // tm),
        in_specs=in_specs,
        out_specs=pl.BlockSpec((1, tm, d), lambda bb, i: (bb, i, 0)),
        out_shape=jax.ShapeDtypeStruct((b, t, d), F32),
        compiler_params=_cparams(("parallel", "parallel")),
        name="outproj",
    )(*os_, w, x, mod, ln_g.reshape(1, d), ln_b.reshape(1, d))


def _delta_kernel(q_ref, k_ref, v_ref, z_ref, bg_ref, gn_ref, s0_ref, o_ref, s_ref, gt_scr,
                  *, heads, nblk, blk, group):
    hg = pl.program_id(1)
    n = pl.program_id(2)

    @pl.when(n == 0)
    def _():
        s_ref[...] = s0_ref[...]

    def head_cols(j):
        return slice(j * C_HEAD_DIM, (j + 1) * C_HEAD_DIM)

    lane = lax.broadcasted_iota(jnp.int32, (blk, LANES), 1)
    ri = lax.broadcasted_iota(jnp.int32, (blk, blk), 0)
    ci = lax.broadcasted_iota(jnp.int32, (blk, blk), 1)
    incl = ci <= ri
    strict = ci < ri
    row2 = lax.broadcasted_iota(jnp.int32, (blk, 2 * blk), 0)
    lane2 = lax.broadcasted_iota(jnp.int32, (blk, 2 * blk), 1)
    hi_lanes = lane2 >= blk
    eye_hi = (lane2 == row2 + blk).astype(F32)
    gn = gn_ref[...]

    rows = [slice(nb * blk, (nb + 1) * blk) for nb in range(nblk)]
    hh_all = range(heads)
    beta_c, g_c, g_last, eg, kb, qk, amat, wq, u = {}, {}, {}, {}, {}, {}, {}, {}, {}

    def prepare(nbs):
        dec = {}
        for nb in nbs:
            bg = bg_ref[0, rows[nb], :]
            gt_scr[nb] = bg.T
            for hh in hh_all:
                it = (nb, hh)
                hidx = hg * heads + hh
                beta_c[it] = jnp.sum(jnp.where(lane == hidx, bg, 0.0), axis=1, keepdims=True)
                g_c[it] = jnp.sum(jnp.where(lane == C_V_HEADS + hidx, bg, 0.0), axis=1, keepdims=True)
                g_last[it] = g_c[it][blk - 1:blk, :]
                eg[it] = jnp.exp(g_c[it])
        yield
        for nb in nbs:
            for hh in hh_all:
                g_r = gt_scr[nb, pl.ds(C_V_HEADS + hg * heads + hh, 1), :]
                dec[(nb, hh)] = jnp.where(incl, jnp.exp(g_c[(nb, hh)] - g_r), 0.0)
        yield
        for nb in nbs:
            for j in range(heads // 2):
                h0, h1 = (nb, 2 * j), (nb, 2 * j + 1)
                kbf = k_ref[0, rows[nb], head_cols(j)]
                k = kbf.astype(F32)
                kb[h0] = k * beta_c[h0]
                kb[h1] = k * beta_c[h1]
                lhs = jnp.concatenate(
                    [q_ref[0, rows[nb], head_cols(j)], kb[h0].astype(BF16), kb[h1].astype(BF16)], axis=0)
                qa = lax.dot_general(lhs, kbf, (((1,), (1,)), ((), ())), preferred_element_type=F32)
                for m, it in enumerate((h0, h1)):
                    qk[it] = (qa[:blk] * dec[it]).astype(BF16)
                    amat[it] = jnp.where(strict, qa[(m + 1) * blk:(m + 2) * blk] * dec[it], 0.0)
        yield

    def invert(nbs):
        its = [(nb, hh) for nb in nbs for hh in hh_all]
        c = {}
        for it in its:
            a = amat.pop(it)
            c[it] = eye_hi - jnp.concatenate([a, jnp.zeros_like(a)], axis=1)
        span = 1
        while span < blk:
            for it in its:
                cb = c[it].astype(BF16)
                c[it] = jnp.dot(cb[:, :blk], cb, preferred_element_type=F32) + jnp.where(hi_lanes, c[it], 0.0)
            yield
            span *= 2
        for it in its:
            t = c[it][:, blk:]
            nb, hh = it
            v = v_ref[0, rows[nb], head_cols(hh)].astype(F32)
            rhs = jnp.concatenate([kb[it] * eg[it], v * beta_c[it]], axis=1).astype(BF16)
            wu = jnp.dot(t.astype(BF16), rhs, preferred_element_type=F32)
            qd = q_ref[0, rows[nb], head_cols(hh // 2)].astype(F32) * eg[it]
            wq[it] = jnp.concatenate([wu[:, :C_HEAD_DIM], qd], axis=0).astype(BF16)
            u[it] = wu[:, C_HEAD_DIM:]
        yield

    s = [s_ref[0, hh] for hh in hh_all]

    def scan(nbs):
        for nb in nbs:
            ws_qs = [jnp.dot(wq.pop((nb, hh)), s[hh].astype(BF16), preferred_element_type=F32) for hh in hh_all]
            yield
            vnb = [(u.pop((nb, hh)) - ws_qs[hh][:blk]).astype(BF16) for hh in hh_all]
            for hh in hh_all:
                it = (nb, hh)
                k = k_ref[0, rows[nb], head_cols(hh // 2)].astype(F32)
                k_dec = (k * jnp.exp(g_last[it] - g_c[it])).astype(BF16)
                s[hh] = s[hh] * jnp.exp(g_last[it]) + lax.dot_general(
                    k_dec, vnb[hh], (((0,), (0,)), ((), ())), preferred_element_type=F32)
            yield
            for hh in hh_all:
                o = ws_qs[hh][blk:] + jnp.dot(qk.pop((nb, hh)), vnb[hh], preferred_element_type=F32)
                on = o * lax.rsqrt(jnp.mean(o * o, axis=-1, keepdims=True) + NORM_EPS) * gn
                hs = head_cols(hh)
                o_ref[0, rows[nb], hs] = (on * _silu(z_ref[0, rows[nb], hs].astype(F32))).astype(o_ref.dtype)
            yield

    def round_robin(gens):
        gens = list(gens)
        while gens:
            for g in list(gens):
                try:
                    next(g)
                except StopIteration:
                    gens.remove(g)

    gsz = min(group, nblk)
    groups = [list(range(g0, g0 + gsz)) for g0 in range(0, nblk, gsz)]
    round_robin([prepare(groups[0])])
    for gi in range(len(groups) + 1):
        active = []
        if gi < len(groups):
            active.append(invert(groups[gi]))
        if gi + 1 < len(groups):
            active.append(prepare(groups[gi + 1]))
        if gi >= 1:
            active.append(scan(groups[gi - 1]))
        round_robin(active)
    for hh in hh_all:
        s_ref[0, hh] = s[hh]


def _delta(p1, bg, s0, o_norm_g, *, heads, nblk, group=4):
    b, t, _ = p1.shape
    blk = CHUNK
    tb = nblk * blk
    cq = heads // 2 * C_HEAD_DIM
    cv = heads * C_HEAD_DIM
    ko, vo, zo = C_QK_WIDTH // cq, 2 * C_QK_WIDTH // cv, C_CONV_CH // cv

    def tok(width, off):
        return pl.BlockSpec((1, tb, width), lambda bb, hg, n: (bb, n, off + hg))

    state_spec = pl.BlockSpec((1, heads, C_HEAD_DIM, C_HEAD_DIM), lambda bb, hg, n: (bb, hg, 0, 0))
    return pl.pallas_call(
        functools.partial(_delta_kernel, heads=heads, nblk=nblk, blk=blk, group=group),
        grid=(b, C_V_HEADS // heads, t // tb),
        in_specs=[tok(cq, 0), tok(cq, ko), tok(cv, vo), tok(cv, zo),
                  pl.BlockSpec((1, tb, LANES), lambda bb, hg, n: (bb, n, 0)),
                  pl.BlockSpec((1, C_HEAD_DIM), lambda bb, hg, n: (0, 0)),
                  state_spec],
        out_specs=[pl.BlockSpec((1, tb, cv), lambda bb, hg, n: (bb, n, hg)), state_spec],
        out_shape=[jax.ShapeDtypeStruct((b, t, C_V_WIDTH), BF16),
                   jax.ShapeDtypeStruct((b, C_V_HEADS, C_HEAD_DIM, C_HEAD_DIM), F32)],
        scratch_shapes=[pltpu.VMEM((nblk, LANES, blk), F32)],
        compiler_params=_cparams(("parallel", "parallel", "arbitrary")),
        name="delta",
    )(p1, p1, p1, p1, bg, o_norm_g.reshape(1, C_HEAD_DIM), s0)


def _trunk(x, mods, wts, cache_k, cache_v, conv_left, s0, *, prompt):
    (w_in_ab, bias_blocks, rel_table, ln_v_g, ln_v_b, w_s, b_s, w_out_ab, w_in_dn, w_extra, w_conv, ad_rows,
     o_norm_g, w_out_dn, ln_g, ln_b) = wts
    b, t, _ = x.shape
    p0 = _inproj(x, mods[0], w_in_ab, tm=PROJ_TM, tn=PROJ_TN, cm=PROJ_CM)
    if prompt:
        o_a = _attn_prompt(p0, bias_blocks)
        o_b = _gmlp(p0, w_s, b_s, ln_v_g, ln_v_b, tb=GMLP_TB, want_vn=False)
        v_n = None
        keep = min(A_BAND * CHUNK, t)
    else:
        o_a = _attn_sample(p0, cache_k, cache_v, rel_table)
        o_b, v_n = _gmlp(p0, w_s, b_s, ln_v_g, ln_v_b, tb=GMLP_TB, want_vn=True)
        keep = t
    new_k = p0[:, t - keep:, A_WIDTH:2 * A_WIDTH].astype(F32).reshape(b, keep, A_HEADS, A_HEAD_DIM)
    new_v = p0[:, t - keep:, 2 * A_WIDTH:3 * A_WIDTH].astype(F32).reshape(b, keep, A_HEADS, A_HEAD_DIM)
    x1 = _outproj([o_a, o_b], w_out_ab, x, mods[0], ln_g[0], ln_b[0], tm=OUT_TM)
    if conv_left is None:
        left8 = jnp.zeros((b, HALO_ROWS, C_CONV_CH), F32)
    else:
        left8 = jnp.pad(conv_left, ((0, 0), (HALO_ROWS - (C_CONV - 1), 0), (0, 0)))
    p1, bg, tail = _inproj(x1, mods[1], w_in_dn, (w_extra, ad_rows, w_conv, left8),
                           tm=PROJ_TM, tn=PROJ_TN, cm=PROJ_CM, cn=PROJ_TN, n_out=C_MAIN)
    new_conv = tail[:, -1, HALO_ROWS - (C_CONV - 1):, :]
    if s0 is None:
        s0 = jnp.zeros((b, C_V_HEADS, C_HEAD_DIM, C_HEAD_DIM), F32)
    t_pad = -t % CHUNK
    if t_pad:
        p1 = jnp.pad(p1, ((0, 0), (0, t_pad), (0, 0)))
        bg = jnp.concatenate([jnp.pad(bg[..., :C_V_HEADS], ((0, 0), (0, t_pad), (0, 0))),
                              jnp.pad(bg[..., C_V_HEADS:], ((0, 0), (0, t_pad), (0, 0)), mode="edge")], axis=-1)
    nblk = max(n for n in (4, 2, 1) if (t + t_pad) % (n * CHUNK) == 0)
    o_c, s_new = _delta(p1, bg, s0, o_norm_g, heads=min(C_V_HEADS, DELTA_CHAINS // nblk), nblk=nblk)
    if t_pad:
        o_c = o_c[:, :t]
    x2 = _outproj([o_c], w_out_dn, x1, mods[1], ln_g[1], ln_b[1], tm=OUT_TM)
    return x2, new_k, new_v, v_n, new_conv, s_new


def kernel(x_prompt, x_sample, cache_a_k, cache_a_v, state_c_conv, state_c_s, c_prompt, c_sample, w_ada, b_ada,
           ln_g, ln_b, w_in_ab, rel_table, ln_v_g, ln_v_b, w_s, b_s, w_out_ab, w_in_dn, w_conv, a_log, dt_bias,
           o_norm_g, w_out_dn):
    bp = c_prompt.shape[0]
    bs = c_sample.shape[0]
    c_all = jnp.concatenate([c_prompt, c_sample], axis=0)
    c_all = jnp.pad(c_all, ((0, -(bp + bs) % 16), (0, 0)))
    ada = _ada(c_all, w_ada, b_ada)
    mods_p = [ada[l, :bp].reshape(bp, 3, D_MODEL) for l in range(DEPTH)]
    mods_s = [ada[l, bp:bp + bs].reshape(bs, 3, D_MODEL) for l in range(DEPTH)]

    extra = w_in_dn[:, C_MAIN:]
    w_extra = jnp.pad(extra, ((0, 0), (0, LANES - extra.shape[1]))).astype(BF16)
    ad_rows = jnp.zeros((2, LANES), F32)
    ad_rows = ad_rows.at[0, C_V_HEADS:2 * C_V_HEADS].set(a_log).at[1, C_V_HEADS:2 * C_V_HEADS].set(dt_bias)
    wts = (w_in_ab.astype(BF16), _band_bias_tile(rel_table), rel_table, ln_v_g, ln_v_b, w_s, b_s,
           w_out_ab.astype(BF16), w_in_dn.astype(BF16), w_extra, w_conv, ad_rows, o_norm_g,
           w_out_dn.astype(BF16), ln_g, ln_b)

    y_p, p_a_k, p_a_v, _, p_c_conv, p_c_s = _trunk(x_prompt, mods_p, wts, None, None, None, None, prompt=True)
    y_s, s_a_k, s_a_v, s_b_v, s_c_conv, s_c_s = _trunk(x_sample, mods_s, wts, cache_a_k, cache_a_v, state_c_conv,
                                                       state_c_s, prompt=False)
    s_b_v = s_b_v.reshape(bs, -1, B_GROUPS, B_GROUP_DIM)
    return (y_p, y_s, p_a_k, p_a_v, p_c_conv, p_c_s, s_a_k, s_a_v, s_b_v, s_c_conv, s_c_s)
```

```python
import functools

import jax
import jax.numpy as jnp
from jax import lax
from jax.experimental import pallas as pl
from jax.experimental.pallas import tpu as pltpu

F32 = jnp.float32
BF16 = jnp.bfloat16

D_MODEL = 2048
DEPTH = 2
CHUNK = 64
A_HEADS = 16
A_HEAD_DIM = 64
A_WIDTH = A_HEADS * A_HEAD_DIM
A_BAND = 8
REL_MAX = 128
B_GROUPS = 8
B_GROUP_DIM = 128
B_WIDTH = B_GROUPS * B_GROUP_DIM
B_CHUNK = 128
C_QK_HEADS = 16
C_V_HEADS = 32
C_HEAD_DIM = 128
C_QK_WIDTH = C_QK_HEADS * C_HEAD_DIM
C_V_WIDTH = C_V_HEADS * C_HEAD_DIM
C_CONV = 4
C_CONV_CH = 2 * C_QK_WIDTH + C_V_WIDTH
C_MAIN = C_CONV_CH + C_V_WIDTH
DN_ALPHA = (2 * DEPTH) ** 0.25
LN_EPS = 1e-5
NORM_EPS = 1e-6
NEG_BIG = -1e30

LANES = 128
HALO_ROWS = 8
VMEM_LIMIT = 56 * 1024 * 1024

PROJ_TM = 1024
PROJ_TN = 1024
PROJ_CM = 256
OUT_TM = 512
GMLP_TB = 512
DELTA_CHAINS = 64


def _cparams(sem):
    return pltpu.CompilerParams(dimension_semantics=sem, vmem_limit_bytes=VMEM_LIMIT)


def _silu(x):
    return x * jax.nn.sigmoid(x)


def _gelu(x):
    return 0.5 * x * (1.0 + lax.erf(x * (2.0 ** -0.5)))


def _layer_norm(x, g, b):
    mu = jnp.mean(x, axis=-1, keepdims=True)
    xc = x - mu
    var = jnp.mean(xc * xc, axis=-1, keepdims=True)
    return xc * lax.rsqrt(var + LN_EPS) * g + b


def _ada_kernel(c_ref, w_ref, b_ref, o_ref):
    c = c_ref[...].astype(BF16)
    w = w_ref[0].astype(BF16)
    o_ref[0] = jnp.dot(c, w, preferred_element_type=F32) + b_ref[0]


def _ada(c_all, w_ada, b_ada):
    rows, d = c_all.shape
    depth, _, n = w_ada.shape
    tn = 768
    return pl.pallas_call(
        _ada_kernel,
        grid=(depth, n // tn),
        in_specs=[
            pl.BlockSpec((rows, d), lambda l, j: (0, 0)),
            pl.BlockSpec((1, d, tn), lambda l, j: (l, 0, j)),
            pl.BlockSpec((1, 1, tn), lambda l, j: (l, 0, j)),
        ],
        out_specs=pl.BlockSpec((1, rows, tn), lambda l, j: (l, 0, j)),
        out_shape=jax.ShapeDtypeStruct((depth, rows, n), F32),
        compiler_params=_cparams(("parallel", "parallel")),
        name="ada",
    )(c_all, w_ada, b_ada.reshape(depth, 1, n))


def _inproj_kernel(*refs, bt, tm, tn, cm, cn, dn):
    if dn:
        x_ref, mod_ref, w_ref, we_ref, ad_ref, wc_ref, cl_ref, o_ref, oe_ref, tail_ref, h_scr, halo_scr = refs
    else:
        x_ref, mod_ref, w_ref, o_ref, h_scr = refs
    i = pl.program_id(1)
    j = pl.program_id(2)
    rows = bt * tm

    @pl.when(j == 0)
    def _():
        h = x_ref[...] * (1.0 + mod_ref[:, 1:2, :]) + mod_ref[:, 0:1, :]
        hb = h.reshape(rows, h.shape[-1]).astype(BF16)
        h_scr[...] = hb
        if dn:
            raw = jnp.dot(hb, we_ref[...], preferred_element_type=F32)
            beta = jax.nn.sigmoid(raw)
            xa = raw + ad_ref[1:2, :]
            softplus = jnp.maximum(xa, 0.0) + jnp.log1p(jnp.exp(-jnp.abs(xa)))
            la = -jnp.exp(ad_ref[0:1, :]) * softplus
            gb = min(CHUNK, tm)
            grp = min(rows, 4 * CHUNK)
            r = lax.broadcasted_iota(jnp.int32, (grp, grp), 0)
            c = lax.broadcasted_iota(jnp.int32, (grp, grp), 1)
            sh = gb.bit_length() - 1
            same = lax.shift_right_logical(r, sh) == lax.shift_right_logical(c, sh)
            csum = ((c <= r) & same).astype(BF16)
            hi = la.astype(BF16)
            rem = la - hi.astype(F32)
            mid = rem.astype(BF16)
            lo = (rem - mid.astype(F32)).astype(BF16)
            g = jnp.concatenate(
                [sum(jnp.dot(csum, part[g0:g0 + grp], preferred_element_type=F32) for part in (hi, mid, lo))
                 for g0 in range(0, rows, grp)], axis=0)
            lane = lax.broadcasted_iota(jnp.int32, raw.shape, 1)
            bg = jnp.where(lane < C_V_HEADS, beta, g)
            oe_ref[...] = bg.reshape(bt, tm, bg.shape[-1])

    def plain():
        acc = jnp.dot(h_scr[...], w_ref[...], preferred_element_type=F32)
        o_ref[...] = acc.reshape(bt, tm, tn).astype(o_ref.dtype)

    if not dn:
        plain()
        return

    n_conv = C_CONV_CH // tn

    def conv_tiles(l2norm):
        qscale = jnp.where(j < C_QK_WIDTH // tn, C_HEAD_DIM ** -0.5, 1.0)
        row8 = lax.broadcasted_iota(jnp.int32, (HALO_ROWS, cn), 0)
        ncp = tn // cn

        def cols(nt):
            return slice(nt * cn, (nt + 1) * cn)

        def epilogue(seg, prev, nt):
            taps = wc_ref[:, cols(nt)]
            y = taps[C_CONV - 1:C_CONV] * seg
            for delay in range(1, C_CONV):
                rolled = pltpu.roll(seg, delay, 0)
                first = jnp.where(row8 < delay, pltpu.roll(prev, delay, 0), rolled[:HALO_ROWS])
                shifted = jnp.concatenate([first, rolled[HALO_ROWS:]], axis=0)
                y = y + taps[C_CONV - 1 - delay:C_CONV - delay] * shifted
            s = _silu(y)
            if not l2norm:
                return s.astype(o_ref.dtype)
            outs = []
            for hd in range(cn // C_HEAD_DIM):
                sg = s[:, hd * C_HEAD_DIM:(hd + 1) * C_HEAD_DIM]
                outs.append(sg * (lax.rsqrt(jnp.sum(sg * sg, axis=-1, keepdims=True) + NORM_EPS) * qscale))
            return jnp.concatenate(outs, axis=1).astype(o_ref.dtype)

        if bt == 1:
            @pl.when(i == 0)
            def _():
                halo_scr[j] = cl_ref[0]

            prev = {nt: halo_scr[j, :, cols(nt)] for nt in range(ncp)}
            pieces = [(c, nt) for c in range(tm // cm) for nt in range(ncp)]
            pending = None
            for p in range(len(pieces) + 1):
                issued = None
                if p < len(pieces):
                    c, nt = pieces[p]
                    issued = (c, nt, jnp.dot(h_scr[c * cm:(c + 1) * cm, :], w_ref[:, cols(nt)],
                                             preferred_element_type=F32))
                if pending is not None:
                    c, nt, a = pending
                    o_ref[0, c * cm:(c + 1) * cm, cols(nt)] = epilogue(a, prev[nt], nt)
                    prev[nt] = a[cm - HALO_ROWS:cm]
                pending = issued
            for nt in range(ncp):
                halo_scr[j, :, cols(nt)] = prev[nt]
                tail_ref[0, 0, :, cols(nt)] = prev[nt]
        else:
            for nt in range(ncp):
                acc = jnp.dot(h_scr[...], w_ref[:, cols(nt)], preferred_element_type=F32)
                for bi in range(bt):
                    seg = acc[bi * tm:(bi + 1) * tm]
                    o_ref[bi, :, cols(nt)] = epilogue(seg, cl_ref[bi, :, cols(nt)], nt)
                    tail_ref[bi, 0, :, cols(nt)] = seg[tm - HALO_ROWS:tm]

    n_qk = (2 * C_QK_WIDTH) // tn
    pl.when(j < n_qk)(functools.partial(conv_tiles, True))
    pl.when((j >= n_qk) & (j < n_conv))(functools.partial(conv_tiles, False))
    pl.when(j >= n_conv)(plain)


def _inproj(x, mod, w, dn_args=None, *, tm, tn, cm, cn=256, n_out=None):
    b, t, d = x.shape
    n = n_out or w.shape[1]
    tm = min(tm, t)
    bt = b if t == tm and b * t <= 1024 else 1
    cm = min(cm, tm)
    dn = dn_args is not None
    in_specs = [
        pl.BlockSpec((bt, tm, d), lambda bb, i, j: (bb, i, 0)),
        pl.BlockSpec((bt, 3, d), lambda bb, i, j: (bb, 0, 0)),
        pl.BlockSpec((d, tn), lambda bb, i, j: (0, j)),
    ]
    out_specs = [pl.BlockSpec((bt, tm, tn), lambda bb, i, j: (bb, i, j))]
    out_shape = [jax.ShapeDtypeStruct((b, t, n), BF16)]
    scratch = [pltpu.VMEM((bt * tm, d), BF16)]
    args = [x, mod, w]
    if dn:
        w_extra, ad_rows, w_conv, conv_left8 = dn_args
        ne = w_extra.shape[1]
        last = C_CONV_CH // tn - 1

        def cj(j):
            return jnp.minimum(j, last)

        in_specs += [
            pl.BlockSpec((d, ne), lambda bb, i, j: (0, 0)),
            pl.BlockSpec((2, ne), lambda bb, i, j: (0, 0)),
            pl.BlockSpec((C_CONV, tn), lambda bb, i, j: (0, cj(j))),
            pl.BlockSpec((bt, HALO_ROWS, tn), lambda bb, i, j: (bb, 0, cj(j))),
        ]
        out_specs += [
            pl.BlockSpec((bt, tm, ne), lambda bb, i, j: (bb, i, 0)),
            pl.BlockSpec((bt, 1, HALO_ROWS, tn), lambda bb, i, j: (bb, i, 0, cj(j))),
        ]
        out_shape += [jax.ShapeDtypeStruct((b, t, ne), F32),
                      jax.ShapeDtypeStruct((b, t // tm, HALO_ROWS, C_CONV_CH), F32)]
        scratch.append(pltpu.VMEM((C_CONV_CH // tn, HALO_ROWS, tn), F32))
        args += [w_extra, ad_rows, w_conv, conv_left8]
    res = pl.pallas_call(
        functools.partial(_inproj_kernel, bt=bt, tm=tm, tn=tn, cm=cm, cn=min(cn, tn), dn=dn),
        grid=(b // bt, t // tm, n // tn),
        in_specs=in_specs,
        out_specs=out_specs,
        out_shape=out_shape,
        scratch_shapes=scratch,
        compiler_params=_cparams(("parallel", "arbitrary", "arbitrary")),
        name="inproj_dn" if dn else "inproj",
    )(*args)
    return res if dn else res[0]


ATTN_TQ = 256
ATTN_SUB = 128
ATTN_WIN = ATTN_SUB + A_BAND * CHUNK
ATTN_HEADS = 16


def _attn_kernel(q_ref, k0_ref, k1_ref, k2_ref, v0_ref, v1_ref, v2_ref, z_ref, bias_ref, o_ref, bias_scr):
    i = pl.program_id(2)
    nsub = ATTN_TQ // ATTN_SUB

    @pl.when(i <= 2)
    def _():
        for sb in range(nsub):
            col = lax.broadcasted_iota(jnp.int32, (1, ATTN_WIN), 1) + sb * ATTN_SUB
            missing = col < (2 - i) * ATTN_TQ
            for h in range(ATTN_HEADS):
                bias_scr[sb * ATTN_HEADS + h] = jnp.where(missing, NEG_BIG, bias_ref[h])

    q = q_ref[0] * (A_HEAD_DIM ** -0.5)
    k = jnp.concatenate([k0_ref[0], k1_ref[0], k2_ref[0]], axis=0)
    v = jnp.concatenate([v0_ref[0], v1_ref[0], v2_ref[0]], axis=0)
    items = [(h, sb) for h in range(ATTN_HEADS) for sb in range(nsub)]
    low = lax.broadcasted_iota(jnp.int32, (1, LANES), 1) < A_HEAD_DIM

    def pair(h):
        return slice(h // 2 * LANES, (h // 2 + 1) * LANES)

    def own(h):
        return low if h % 2 == 0 else jnp.logical_not(low)

    q_own = [jnp.where(own(h), q[:, pair(h)], jnp.zeros((), BF16)) for h in range(ATTN_HEADS)]
    v_ext = [jnp.where(own(h), v[:, pair(h)], jnp.ones((), BF16)) for h in range(ATTN_HEADS)]

    def scores(it):
        h, sb = it
        qs = q_own[h][sb * ATTN_SUB:(sb + 1) * ATTN_SUB]
        ks = k[sb * ATTN_SUB:sb * ATTN_SUB + ATTN_WIN, pair(h)]
        s = lax.dot_general(qs, ks, (((1,), (1,)), ((), ())), preferred_element_type=F32)
        return s + bias_scr[sb * ATTN_HEADS + h]

    def weights(s):
        return jnp.exp(s - jnp.max(s, axis=-1, keepdims=True)).astype(BF16)

    def attend(it, p):
        h, sb = it
        return jnp.dot(p, v_ext[h][sb * ATTN_SUB:sb * ATTN_SUB + ATTN_WIN], preferred_element_type=F32)

    ss = [scores(it) for it in items]
    ps = [weights(s) for s in ss]
    ol = {it: attend(it, p) for it, p in zip(items, ps)}

    def normalised(h0, sb):
        e0, e1 = ol[(h0, sb)], ol[(h0 + 1, sb)]
        return jnp.where(low, e0 / pltpu.roll(e0, A_HEAD_DIM, 1), e1 / pltpu.roll(e1, A_HEAD_DIM, 1))

    o = jnp.concatenate(
        [jnp.concatenate([normalised(h0, sb) for sb in range(nsub)], axis=0) for h0 in range(0, ATTN_HEADS, 2)],
        axis=1)
    o_ref[0] = (o * _silu(z_ref[0].astype(F32))).astype(o_ref.dtype)


def _band_bias_tile(rel_table):
    h = rel_table.shape[0]
    far = A_BAND * CHUNK
    n_diag = ATTN_WIN + ATTN_SUB - 1
    n_const = far + ATTN_SUB - 1 - REL_MAX + 1
    w = jnp.concatenate([jnp.broadcast_to(rel_table[:, 2 * REL_MAX:], (h, n_const)),
                         rel_table[:, 2 * REL_MAX - 1::-1][:, :n_diag - n_const]], axis=1).astype(F32)
    tiled =jnp.tile(jnp.pad(w, ((0, 0), (0, 1))), (1, ATTN_SUB))[:, :ATTN_SUB * n_diag]
    toep = tiled.reshape(h, ATTN_SUB, n_diag)[:, :, ATTN_SUB - 1:ATTN_SUB - 1 + ATTN_WIN]
    row = jnp.arange(ATTN_SUB)[:, None]
    colx = jnp.arange(ATTN_WIN)[None, :]
    rel = colx - (row // CHUNK) * CHUNK
    band = (rel >= 0) & (rel < (A_BAND + 1) * CHUNK)
    return jnp.where(band[None], toep, NEG_BIG)


def _attn_prompt(p0, bias):
    b, t, _ = p0.shape
    tq = ATTN_TQ
    width = ATTN_HEADS * A_HEAD_DIM
    per = A_WIDTH // width
    qo, ko, vo, zo = 0, per, 2 * per, 3 * per

    def blk(off, back):
        return pl.BlockSpec((1, tq, width), lambda hg, bb, i: (bb, jnp.maximum(i - back, 0), off + hg))

    return pl.pallas_call(
        _attn_kernel,
        grid=(per, b, t // tq),
        in_specs=[blk(qo, 0), blk(ko, 2), blk(ko, 1), blk(ko, 0), blk(vo, 2), blk(vo, 1), blk(vo, 0), blk(zo, 0),
                  pl.BlockSpec((ATTN_HEADS, ATTN_SUB, ATTN_WIN), lambda hg, bb, i: (hg, 0, 0))],
        out_specs=pl.BlockSpec((1, tq, width), lambda hg, bb, i: (bb, i, hg)),
        out_shape=jax.ShapeDtypeStruct((b, t, A_WIDTH), BF16),
        scratch_shapes=[pltpu.VMEM((ATTN_TQ // ATTN_SUB * ATTN_HEADS, ATTN_SUB, ATTN_WIN), F32)],
        compiler_params=_cparams(("parallel", "parallel", "arbitrary")),
        name="attn_prompt",
    )(p0, p0, p0, p0, p0, p0, p0, p0, bias)


def _attn_sample_kernel(q_ref, k_ref, v_ref, z_ref, ck_ref, cv_ref, bias_ref, o_ref, *, pad_rows):
    q = q_ref[0]
    zpad = jnp.zeros((pad_rows, A_WIDTH), BF16)
    k = jnp.concatenate([ck_ref[0].astype(BF16), k_ref[0], zpad], axis=0)
    v = jnp.concatenate([cv_ref[0].astype(BF16), v_ref[0], zpad], axis=0)
    def head(h):
        return slice(h * A_HEAD_DIM, (h + 1) * A_HEAD_DIM)

    ss = [lax.dot_general(q[:, head(h)], k[:, head(h)], (((1,), (1,)), ((), ())), preferred_element_type=F32)
          * (A_HEAD_DIM ** -0.5) + bias_ref[h] for h in range(A_HEADS)]
    ps = [jnp.exp(s - jnp.max(s, axis=-1, keepdims=True)) for s in ss]
    outs = [jnp.dot(p.astype(BF16), v[:, head(h)], preferred_element_type=F32) / jnp.sum(p, axis=-1, keepdims=True)
            for h, p in enumerate(ps)]
    o = jnp.concatenate(outs, axis=1)
    o_ref[0] = (o * _silu(z_ref[0].astype(F32))).astype(o_ref.dtype)


def _attn_sample(p0, cache_k, cache_v, rel_table):
    b, t, _ = p0.shape
    n_cache = cache_k.shape[1]
    n_keys = n_cache + t
    n_pad = -n_keys % LANES
    dist = jnp.arange(t)[:, None] + n_cache - jnp.arange(n_keys)[None, :]
    bias = rel_table[:, jnp.clip(dist, -REL_MAX, REL_MAX) + REL_MAX].astype(F32)
    bias = jnp.pad(bias, ((0, 0), (0, 0), (0, n_pad)), constant_values=NEG_BIG)
    ck = cache_k.reshape(b, n_cache, A_WIDTH)
    cv = cache_v.reshape(b, n_cache, A_WIDTH)

    def col(j):
        return pl.BlockSpec((1, t, A_WIDTH), lambda bb: (bb, 0, j))

    cache_spec = pl.BlockSpec((1, n_cache, A_WIDTH), lambda bb: (bb, 0, 0))
    return pl.pallas_call(
        functools.partial(_attn_sample_kernel, pad_rows=n_pad),
        grid=(b,),
        in_specs=[col(0), col(1), col(2), col(3), cache_spec, cache_spec,
                  pl.BlockSpec((A_HEADS, t, n_keys + n_pad), lambda bb: (0, 0, 0))],
        out_specs=pl.BlockSpec((1, t, A_WIDTH), lambda bb: (bb, 0, 0)),
        out_shape=jax.ShapeDtypeStruct((b, t, A_WIDTH), BF16),
        compiler_params=_cparams(("parallel",)),
        name="attn_sample",
    )(p0, p0, p0, p0, ck, cv, bias)


def _gmlp_kernel(u_ref, v_ref, z_ref, ws_ref, bst_ref, g_ref, b_ref, o_ref, *vn_out, n_mix, tb):
    vn = _layer_norm(_gelu(v_ref[0].astype(F32)), g_ref[...], b_ref[...])
    if vn_out:
        vn_out[0][0] = vn
    z = z_ref[0].astype(F32)
    gate = _gelu(u_ref[0].astype(F32)) * _silu(z)
    row = lax.broadcasted_iota(jnp.int32, (B_CHUNK, B_CHUNK), 0)
    colm = lax.broadcasted_iota(jnp.int32, (B_CHUNK, B_CHUNK), 1)
    keep = (colm <= row) & (row < n_mix)
    rows = min(tb, B_CHUNK)
    for g in range(B_GROUPS):
        gs = slice(g * B_GROUP_DIM, (g + 1) * B_GROUP_DIM)
        w = jnp.where(keep, ws_ref[g], 0.0).astype(BF16)
        bias = bst_ref[:, g:g + 1]
        for c in range(max(tb // B_CHUNK, 1)):
            rs = slice(c * B_CHUNK, c * B_CHUNK + rows)
            vg = vn[rs, gs].astype(BF16)
            if rows < B_CHUNK:
                vg = jnp.concatenate([vg, jnp.zeros((B_CHUNK - rows, B_GROUP_DIM), BF16)], axis=0)
            mix = jnp.dot(w, vg, preferred_element_type=F32) + bias
            o_ref[0, rs, gs] = (gate[rs, gs] * mix[:rows]).astype(o_ref.dtype)


def _gmlp(p0, w_s, b_s, ln_v_g, ln_v_b, *, tb, want_vn):
    b, t, _ = p0.shape
    tb = min(tb, t)
    n_mix = min(t, B_CHUNK)
    base = 4 * A_WIDTH // B_WIDTH

    def col(j):
        return pl.BlockSpec((1, tb, B_WIDTH), lambda bb, i: (bb, i, base + j))

    out_specs = [pl.BlockSpec((1, tb, B_WIDTH), lambda bb, i: (bb, i, 0))]
    out_shape = [jax.ShapeDtypeStruct((b, t, B_WIDTH), BF16)]
    if want_vn:
        out_specs.append(pl.BlockSpec((1, tb, B_WIDTH), lambda bb, i: (bb, i, 0)))
        out_shape.append(jax.ShapeDtypeStruct((b, t, B_WIDTH), F32))
    res = pl.pallas_call(
        functools.partial(_gmlp_kernel, n_mix=n_mix, tb=tb),
        grid=(b, t // tb),
        in_specs=[col(0), col(1), col(2),
                  pl.BlockSpec((B_GROUPS, B_CHUNK, B_CHUNK), lambda bb, i: (0, 0, 0)),
                  pl.BlockSpec((B_CHUNK, B_GROUPS), lambda bb, i: (0, 0)),
                  pl.BlockSpec((1, B_WIDTH), lambda bb, i: (0, 0)),
                  pl.BlockSpec((1, B_WIDTH), lambda bb, i: (0, 0))],
        out_specs=out_specs,
        out_shape=out_shape,
        compiler_params=_cparams(("parallel", "parallel")),
        name="gmlp",
    )(p0, p0, p0, w_s, b_s.T, ln_v_g.reshape(1, -1), ln_v_b.reshape(1, -1))
    return res if want_vn else res[0]


def _outproj_kernel(*refs, widths):
    o_refs = refs[:len(widths)]
    w_ref, x_ref, mod_ref, g_ref, b_ref, out_ref = refs[len(widths):]
    o = o_refs[0][0] if len(widths) == 1 else jnp.concatenate([o_ref[0] for o_ref in o_refs], axis=1)
    y = jnp.dot(o, w_ref[...], preferred_element_type=F32)
    r = DN_ALPHA * x_ref[0] + (1.0 + mod_ref[0, 2:3, :]) * y
    out_ref[0] = _layer_norm(r, g_ref[...], b_ref[...])


def _outproj(os_, w, x, mod, ln_g, ln_b, *, tm):
    b, t, d = x.shape
    tm = min(tm, t)
    widths = tuple(o.shape[-1] for o in os_)
    ktot = sum(widths)
    in_specs = [pl.BlockSpec((1, tm, kw), lambda bb, i: (bb, i, 0)) for kw in widths]
    in_specs += [
        pl.BlockSpec((ktot, d), lambda bb, i: (0, 0), pipeline_mode=pl.Buffered(1)),
        pl.BlockSpec((1, tm, d), lambda bb, i: (bb, i, 0)),
        pl.BlockSpec((1, 3, d), lambda bb, i: (bb, 0, 0)),
        pl.BlockSpec((1, d), lambda bb, i: (0, 0)),
        pl.BlockSpec((1, d), lambda bb, i: (0, 0)),
    ]
    return pl.pallas_call(
        functools.partial(_outproj_kernel, widths=widths),
        grid=(b, t // tm),
        in_specs=in_specs,
        out_specs=pl.BlockSpec((1, tm, d), lambda bb, i: (bb, i, 0)),
        out_shape=jax.ShapeDtypeStruct((b, t, d), F32),
        compiler_params=_cparams(("parallel", "parallel")),
        name="outproj",
    )(*os_, w, x, mod, ln_g.reshape(1, d), ln_b.reshape(1, d))


def _delta_kernel(q_ref, k_ref, v_ref, z_ref, bg_ref, gn_ref, s0_ref, o_ref, s_ref, gt_scr,
                  *, heads, nblk, blk, group):
    hg = pl.program_id(1)
    n = pl.program_id(2)

    @pl.when(n == 0)
    def _():
        s_ref[...] = s0_ref[...]

    def head_cols(j):
        return slice(j * C_HEAD_DIM, (j + 1) * C_HEAD_DIM)

    lane = lax.broadcasted_iota(jnp.int32, (blk, LANES), 1)
    ri = lax.broadcasted_iota(jnp.int32, (blk, blk), 0)
    ci = lax.broadcasted_iota(jnp.int32, (blk, blk), 1)
    incl = ci <= ri
    strict = ci < ri
    row2 = lax.broadcasted_iota(jnp.int32, (blk, 2 * blk), 0)
    lane2 = lax.broadcasted_iota(jnp.int32, (blk, 2 * blk), 1)
    hi_lanes = lane2 >= blk
    eye_hi = (lane2 == row2 + blk).astype(F32)
    gn = gn_ref[...]

    rows = [slice(nb * blk, (nb + 1) * blk) for nb in range(nblk)]
    hh_all = range(heads)
    beta_c, g_c, g_last, eg, kb, qk, amat, wq, u = {}, {}, {}, {}, {}, {}, {}, {}, {}

    def prepare(nbs):
        dec = {}
        for nb in nbs:
            bg = bg_ref[0, rows[nb], :]
            gt_scr[nb] = bg.T
            for hh in hh_all:
                it = (nb, hh)
                hidx = hg * heads + hh
                beta_c[it] = jnp.sum(jnp.where(lane == hidx, bg, 0.0), axis=1, keepdims=True)
                g_c[it] = jnp.sum(jnp.where(lane == C_V_HEADS + hidx, bg, 0.0), axis=1, keepdims=True)
                g_last[it] = g_c[it][blk - 1:blk, :]
                eg[it] = jnp.exp(g_c[it])
        yield
        for nb in nbs:
            for hh in hh_all:
                g_r = gt_scr[nb, pl.ds(C_V_HEADS + hg * heads + hh, 1), :]
                dec[(nb, hh)] = jnp.where(incl, jnp.exp(g_c[(nb, hh)] - g_r), 0.0)
        yield
        for nb in nbs:
            for j in range(heads // 2):
                h0, h1 = (nb, 2 * j), (nb, 2 * j + 1)
                kbf = k_ref[0, rows[nb], head_cols(j)]
                k = kbf.astype(F32)
                kb[h0] = k * beta_c[h0]
                kb[h1] = k * beta_c[h1]
                lhs = jnp.concatenate(
                    [q_ref[0, rows[nb], head_cols(j)], kb[h0].astype(BF16), kb[h1].astype(BF16)], axis=0)
                qa = lax.dot_general(lhs, kbf, (((1,), (1,)), ((), ())), preferred_element_type=F32)
                for m, it in enumerate((h0, h1)):
                    qk[it] = (qa[:blk] * dec[it]).astype(BF16)
                    amat[it] = jnp.where(strict, qa[(m + 1) * blk:(m + 2) * blk] * dec[it], 0.0)
        yield

    def invert(nbs):
        its = [(nb, hh) for nb in nbs for hh in hh_all]
        c = {}
        for it in its:
            a = amat.pop(it)
            c[it] = eye_hi - jnp.concatenate([a, jnp.zeros_like(a)], axis=1)
        span = 1
        while span < blk:
            for it in its:
                cb = c[it].astype(BF16)
                c[it] = jnp.dot(cb[:, :blk], cb, preferred_element_type=F32) + jnp.where(hi_lanes, c[it], 0.0)
            yield
            span *= 2
        for it in its:
            t = c[it][:, blk:]
            nb, hh = it
            v = v_ref[0, rows[nb], head_cols(hh)].astype(F32)
            rhs = jnp.concatenate([kb[it] * eg[it], v * beta_c[it]], axis=1).astype(BF16)
            wu = jnp.dot(t.astype(BF16), rhs, preferred_element_type=F32)
            qd = q_ref[0, rows[nb], head_cols(hh // 2)].astype(F32) * eg[it]
            wq[it] = jnp.concatenate([wu[:, :C_HEAD_DIM], qd], axis=0).astype(BF16)
            u[it] = wu[:, C_HEAD_DIM:]
        yield

    s = [s_ref[0, hh] for hh in hh_all]

    def scan(nbs):
        for nb in nbs:
            ws_qs = [jnp.dot(wq.pop((nb, hh)), s[hh].astype(BF16), preferred_element_type=F32) for hh in hh_all]
            yield
            vnb = [(u.pop((nb, hh)) - ws_qs[hh][:blk]).astype(BF16) for hh in hh_all]
            for hh in hh_all:
                it = (nb, hh)
                k = k_ref[0, rows[nb], head_cols(hh // 2)].astype(F32)
                k_dec = (k * jnp.exp(g_last[it] - g_c[it])).astype(BF16)
                s[hh] = s[hh] * jnp.exp(g_last[it]) + lax.dot_general(
                    k_dec, vnb[hh], (((0,), (0,)), ((), ())), preferred_element_type=F32)
            yield
            for hh in hh_all:
                o = ws_qs[hh][blk:] + jnp.dot(qk.pop((nb, hh)), vnb[hh], preferred_element_type=F32)
                on = o * lax.rsqrt(jnp.mean(o * o, axis=-1, keepdims=True) + NORM_EPS) * gn
                hs = head_cols(hh)
                o_ref[0, rows[nb], hs] = (on * _silu(z_ref[0, rows[nb], hs].astype(F32))).astype(o_ref.dtype)
            yield

    def round_robin(gens):
        gens = list(gens)
        while gens:
            for g in list(gens):
                try:
                    next(g)
                except StopIteration:
                    gens.remove(g)

    gsz = min(group, nblk)
    groups = [list(range(g0, g0 + gsz)) for g0 in range(0, nblk, gsz)]
    round_robin([prepare(groups[0])])
    for gi in range(len(groups) + 1):
        active = []
        if gi < len(groups):
            active.append(invert(groups[gi]))
        if gi + 1 < len(groups):
            active.append(prepare(groups[gi + 1]))
        if gi >= 1:
            active.append(scan(groups[gi - 1]))
        round_robin(active)
    for hh in hh_all:
        s_ref[0, hh] = s[hh]


def _delta(p1, bg, s0, o_norm_g, *, heads, nblk, group=4):
    b, t, _ = p1.shape
    blk = CHUNK
    tb = nblk * blk
    cq = heads // 2 * C_HEAD_DIM
    cv = heads * C_HEAD_DIM
    ko, vo, zo = C_QK_WIDTH // cq, 2 * C_QK_WIDTH // cv, C_CONV_CH // cv

    def tok(width, off):
        return pl.BlockSpec((1, tb, width), lambda bb, hg, n: (bb, n, off + hg))

    state_spec = pl.BlockSpec((1, heads, C_HEAD_DIM, C_HEAD_DIM), lambda bb, hg, n: (bb, hg, 0, 0))
    return pl.pallas_call(
        functools.partial(_delta_kernel, heads=heads, nblk=nblk, blk=blk, group=group),
        grid=(b, C_V_HEADS // heads, t // tb),
        in_specs=[tok(cq, 0), tok(cq, ko), tok(cv, vo), tok(cv, zo),
                  pl.BlockSpec((1, tb, LANES), lambda bb, hg, n: (bb, n, 0)),
                  pl.BlockSpec((1, C_HEAD_DIM), lambda bb, hg, n: (0, 0)),
                  state_spec],
        out_specs=[pl.BlockSpec((1, tb, cv), lambda bb, hg, n: (bb, n, hg)), state_spec],
        out_shape=[jax.ShapeDtypeStruct((b, t, C_V_WIDTH), BF16),
                   jax.ShapeDtypeStruct((b, C_V_HEADS, C_HEAD_DIM, C_HEAD_DIM), F32)],
        scratch_shapes=[pltpu.VMEM((nblk, LANES, blk), F32)],
        compiler_params=_cparams(("parallel", "parallel", "arbitrary")),
        name="delta",
    )(p1, p1, p1, p1, bg, o_norm_g.reshape(1, C_HEAD_DIM), s0)


def _trunk(x, mods, wts, cache_k, cache_v, conv_left, s0, *, prompt):
    (w_in_ab, bias_blocks, rel_table, ln_v_g, ln_v_b, w_s, b_s, w_out_ab, w_in_dn, w_extra, w_conv, ad_rows,
     o_norm_g, w_out_dn, ln_g, ln_b) = wts
    b, t, _ = x.shape
    p0 = _inproj(x, mods[0], w_in_ab, tm=PROJ_TM, tn=PROJ_TN, cm=PROJ_CM)
    if prompt:
        o_a = _attn_prompt(p0, bias_blocks)
        o_b = _gmlp(p0, w_s, b_s, ln_v_g, ln_v_b, tb=GMLP_TB, want_vn=False)
        v_n = None
        keep = min(A_BAND * CHUNK, t)
    else:
        o_a = _attn_sample(p0, cache_k, cache_v, rel_table)
        o_b, v_n = _gmlp(p0, w_s, b_s, ln_v_g, ln_v_b, tb=GMLP_TB, want_vn=True)
        keep = t
    new_k = p0[:, t - keep:, A_WIDTH:2 * A_WIDTH].astype(F32).reshape(b, keep, A_HEADS, A_HEAD_DIM)
    new_v = p0[:, t - keep:, 2 * A_WIDTH:3 * A_WIDTH].astype(F32).reshape(b, keep, A_HEADS, A_HEAD_DIM)
    x1 = _outproj([o_a, o_b], w_out_ab, x, mods[0], ln_g[0], ln_b[0], tm=OUT_TM)
    if conv_left is None:
        left8 = jnp.zeros((b, HALO_ROWS, C_CONV_CH), F32)
    else:
        left8 = jnp.pad(conv_left, ((0, 0), (HALO_ROWS - (C_CONV - 1), 0), (0, 0)))
    p1, bg, tail = _inproj(x1, mods[1], w_in_dn, (w_extra, ad_rows, w_conv, left8),
                           tm=PROJ_TM, tn=PROJ_TN, cm=PROJ_CM, cn=PROJ_TN, n_out=C_MAIN)
    new_conv = tail[:, -1, HALO_ROWS - (C_CONV - 1):, :]
    if s0 is None:
        s0 = jnp.zeros((b, C_V_HEADS, C_HEAD_DIM, C_HEAD_DIM), F32)
    t_pad = -t % CHUNK
    if t_pad:
        p1 = jnp.pad(p1, ((0, 0), (0, t_pad), (0, 0)))
        bg = jnp.concatenate([jnp.pad(bg[..., :C_V_HEADS], ((0, 0), (0, t_pad), (0, 0))),
                              jnp.pad(bg[..., C_V_HEADS:], ((0, 0), (0, t_pad), (0, 0)), mode="edge")], axis=-1)
    nblk = max(n for n in (4, 2, 1) if (t + t_pad) % (n * CHUNK) == 0)
    o_c, s_new = _delta(p1, bg, s0, o_norm_g, heads=min(C_V_HEADS, DELTA_CHAINS // nblk), nblk=nblk)
    if t_pad:
        o_c = o_c[:, :t]
    x2 = _outproj([o_c], w_out_dn, x1, mods[1], ln_g[1], ln_b[1], tm=OUT_TM)
    return x2, new_k, new_v, v_n, new_conv, s_new


def kernel(x_prompt, x_sample, cache_a_k, cache_a_v, state_c_conv, state_c_s, c_prompt, c_sample, w_ada, b_ada,
           ln_g, ln_b, w_in_ab, rel_table, ln_v_g, ln_v_b, w_s, b_s, w_out_ab, w_in_dn, w_conv, a_log, dt_bias,
           o_norm_g, w_out_dn):
    bp = c_prompt.shape[0]
    bs = c_sample.shape[0]
    c_all = jnp.concatenate([c_prompt, c_sample], axis=0)
    c_all = jnp.pad(c_all, ((0, -(bp + bs) % 16), (0, 0)))
    ada = _ada(c_all, w_ada, b_ada)
    mods_p = [ada[l, :bp].reshape(bp, 3, D_MODEL) for l in range(DEPTH)]
    mods_s = [ada[l, bp:bp + bs].reshape(bs, 3, D_MODEL) for l in range(DEPTH)]

    extra = w_in_dn[:, C_MAIN:]
    w_extra = jnp.pad(extra, ((0, 0), (0, LANES - extra.shape[1]))).astype(BF16)
    ad_rows = jnp.zeros((2, LANES), F32)
    ad_rows = ad_rows.at[0, C_V_HEADS:2 * C_V_HEADS].set(a_log).at[1, C_V_HEADS:2 * C_V_HEADS].set(dt_bias)
    wts = (w_in_ab.astype(BF16), _band_bias_tile(rel_table), rel_table, ln_v_g, ln_v_b, w_s, b_s,
           w_out_ab.astype(BF16), w_in_dn.astype(BF16), w_extra, w_conv, ad_rows, o_norm_g,
           w_out_dn.astype(BF16), ln_g, ln_b)

    y_p, p_a_k, p_a_v, _, p_c_conv, p_c_s = _trunk(x_prompt, mods_p, wts, None, None, None, None, prompt=True)
    y_s, s_a_k, s_a_v, s_b_v, s_c_conv, s_c_s = _trunk(x_sample, mods_s, wts, cache_a_k, cache_a_v, state_c_conv,
                                                       state_c_s, prompt=False)
    s_b_v = s_b_v.reshape(bs, -1, B_GROUPS, B_GROUP_DIM)
    return (y_p, y_s, p_a_k, p_a_v, p_c_conv, p_c_s, s_a_k, s_a_v, s_b_v, s_c_conv, s_c_s)
```
